```python
import math, functools
import jax, jax.numpy as jnp
from jax import lax
import numpy as np

D_MODEL = 2048
BATCH = 4
SEQ = 2048
DEPTH = 4
DEC_BATCH = 8
DEC_SEQ = 4
PAST_LEN = 16384
PAGE_SIZE = 128

MIX_WIDTH = D_MODEL
RET_WIDTH = MIX_WIDTH // 2
MOBA_WIDTH = MIX_WIDTH - RET_WIDTH
HEAD_DIM = 128
RET_HEADS = RET_WIDTH // HEAD_DIM
MOBA_HEADS = MOBA_WIDTH // HEAD_DIM
IN_COLS = 4 * RET_WIDTH + 4 * MOBA_WIDTH
RET_CHUNK = 128
RET_THETA = 10000.0
MOBA_BLOCK = 256
MOBA_TOP_K = 3
MOBA_Q_CHUNK = 16
ROPE_THETA = 500000.0
ROT_DIM = HEAD_DIM // 4
PLE_DIM = 256
NORM_EPS = 1e-6

kernel_name = "hymba_retention_moba_decoder_step"


def rms_norm(x, w):
    xf = x.astype(jnp.float32)
    y = xf * lax.rsqrt(jnp.mean(xf * xf, axis=-1, keepdims=True) + NORM_EPS)
    return y.astype(x.dtype) * w


def rotary(x, pos, rot_dim, theta):
    half = rot_dim // 2
    freqs = theta ** (-jnp.arange(half, dtype=jnp.float32) / half)
    ang = pos.astype(jnp.float32)[:, None] * freqs[None, :]
    cos = jnp.cos(ang)[None, :, None, :].astype(x.dtype)
    sin = jnp.sin(ang)[None, :, None, :].astype(x.dtype)
    x1, x2 = x[..., :half], x[..., half:rot_dim]
    return jnp.concatenate([x1 * cos - x2 * sin, x1 * sin + x2 * cos, x[..., rot_dim:]], axis=-1)


def split_heads(z, n_heads):
    b, l, _ = z.shape
    return z.reshape(b, l, n_heads, -1)


def input_projection(x, norm_w, w_in):
    z = rms_norm(x, norm_w) @ w_in
    cuts = np.cumsum([RET_WIDTH] * 4 + [MOBA_WIDTH] * 3).tolist()
    return jnp.split(z, cuts, axis=-1)


def retention_chunkwise(q, k, v, s0):
    b, h, l, _ = q.shape
    c = math.gcd(l, RET_CHUNK)
    n = l // c
    log_g = jnp.log(1.0 - 2.0 ** (-5.0 - jnp.arange(RET_HEADS, dtype=jnp.float32)))
    i = jnp.arange(c, dtype=jnp.float32)
    rel = i[:, None] - i[None, :]
    decay_in = jnp.where(rel >= 0, jnp.exp(jnp.maximum(rel, 0.0)[None] * log_g[:, None, None]), 0.0)
    q_dec = jnp.exp((i + 1.0)[None, :] * log_g[:, None])[..., None]
    k_dec = jnp.exp((c - 1.0 - i)[None, :] * log_g[:, None])[..., None]
    chunk_dec = jnp.exp(c * log_g)[:, None, None]

    def to_chunks(t):
        return t.reshape(b, h, n, c, t.shape[-1]).transpose(2, 0, 1, 3, 4)

    def step(s, xs):
        qc, kc, vc = xs
        scores = jnp.einsum('bhid,bhjd->bhij', qc, kc) * decay_in
        o = jnp.einsum('bhij,bhje->bhie', scores, vc) + jnp.einsum('bhid,bhde->bhie', qc * q_dec, s)
        s = s * chunk_dec + jnp.einsum('bhjd,bhje->bhde', kc * k_dec, vc)
        return s, o

    s_fin, o = lax.scan(step, s0, (to_chunks(q), to_chunks(k), to_chunks(v)))
    o = o.transpose(1, 2, 0, 3, 4).reshape(b, h, l, -1)
    return o, s_fin


def retention_branch(rq, rk, rv, pos, s0):
    q = rotary(split_heads(rq, RET_HEADS), pos, HEAD_DIM, RET_THETA)
    k = rotary(split_heads(rk, RET_HEADS), pos, HEAD_DIM, RET_THETA) * (HEAD_DIM ** -0.5)
    v = split_heads(rv, RET_HEADS)
    f32 = lambda t: t.astype(jnp.float32).transpose(0, 2, 1, 3)
    o, s = retention_chunkwise(f32(q), f32(k), f32(v), s0.astype(jnp.float32))
    mu = jnp.mean(o, axis=-1, keepdims=True)
    var = jnp.mean(jnp.square(o - mu), axis=-1, keepdims=True)
    o = (o - mu) * lax.rsqrt(var + NORM_EPS)
    b, h, l, d = o.shape
    o = o.transpose(0, 2, 1, 3).reshape(b, l, h * d).astype(rq.dtype)
    return o, s.astype(s0.dtype)


def moba_qkv(mq, mk, mv, pos):
    q = rotary(split_heads(mq, MOBA_HEADS), pos, ROT_DIM, ROPE_THETA).transpose(0, 2, 1, 3)
    k = rotary(split_heads(mk, MOBA_HEADS), pos, ROT_DIM, ROPE_THETA).transpose(0, 2, 1, 3)
    v = split_heads(mv, MOBA_HEADS).transpose(0, 2, 1, 3)
    return q, k, v


def moba_prompt(q, k, v):
    b, h, s, d = q.shape
    scale = d ** -0.5
    nb = -(-s // MOBA_BLOCK)
    pad = nb * MOBA_BLOCK - s
    kp = jnp.pad(k, ((0, 0), (0, 0), (0, pad), (0, 0)))
    vp = jnp.pad(v, ((0, 0), (0, 0), (0, pad), (0, 0)))
    kb = kp.reshape(b, h, nb, MOBA_BLOCK, d)
    vb = vp.reshape(b, h, nb, MOBA_BLOCK, d)
    n_sel = min(MOBA_TOP_K, nb - 1)
    if n_sel > 0:
        k_mean = jnp.mean(kb.astype(jnp.float32), axis=3)
        gate = jnp.einsum('bhsd,bhnd->bhsn', q.astype(jnp.float32), k_mean)
        q_blk = jnp.arange(s) // MOBA_BLOCK
        fully_past = jnp.arange(nb)[None, :] < q_blk[:, None]
        gate = jnp.where(fully_past, gate, -jnp.inf)
        top_val, top_idx = lax.top_k(gate, n_sel)
        sel_ok = jnp.isfinite(top_val)
    else:
        top_idx = jnp.zeros((b, h, s, 0), jnp.int32)
        sel_ok = jnp.zeros((b, h, s, 0), bool)
    qc_len = MOBA_Q_CHUNK
    nc = s // qc_len
    chunked = lambda t: t.reshape(b, h, nc, qc_len, t.shape[-1]).transpose(2, 0, 1, 3, 4)
    bi = jnp.arange(b)[:, None, None, None]
    hi = jnp.arange(h)[None, :, None, None]
    n_key_sel = n_sel * MOBA_BLOCK

    def step(xs):
        ci, q_i, idx_i, ok_i = xs
        start = ci * qc_len
        blk = start // MOBA_BLOCK
        k_own = lax.dynamic_slice_in_dim(kp, blk * MOBA_BLOCK, MOBA_BLOCK, axis=2)
        v_own = lax.dynamic_slice_in_dim(vp, blk * MOBA_BLOCK, MOBA_BLOCK, axis=2)
        q_pos = start + jnp.arange(qc_len)
        k_pos = blk * MOBA_BLOCK + jnp.arange(MOBA_BLOCK)
        s_own = jnp.einsum('bhqd,bhkd->bhqk', q_i, k_own).astype(jnp.float32) * scale
        s_own = jnp.where(k_pos[None, :] <= q_pos[:, None], s_own, -jnp.inf)
        k_sel = kb[bi, hi, idx_i]
        v_sel = vb[bi, hi, idx_i]
        s_sel = jnp.einsum('bhqd,bhqnkd->bhqnk', q_i, k_sel).astype(jnp.float32) * scale
        s_sel = jnp.where(ok_i[..., None], s_sel, -jnp.inf).reshape(b, h, qc_len, n_key_sel)
        p = jax.nn.softmax(jnp.concatenate([s_sel, s_own], axis=-1), axis=-1).astype(v.dtype)
        p_sel = p[..., :n_key_sel].reshape(b, h, qc_len, n_sel, MOBA_BLOCK)
        return (jnp.einsum('bhqnk,bhqnkd->bhqd', p_sel, v_sel)
                + jnp.einsum('bhqk,bhkd->bhqd', p[..., n_key_sel:], v_own))

    o = lax.map(step, (jnp.arange(nc), chunked(q), chunked(top_idx), chunked(sel_ok)))
    return o.transpose(1, 2, 0, 3, 4).reshape(b, h, s, d)


def moba_sample(q, k, v, k_pool, v_pool, page_table):
    b, h, t, d = q.shape
    scale = d ** -0.5
    n_pages = page_table.shape[1]
    ppb = MOBA_BLOCK // PAGE_SIZE
    n_full = (n_pages * PAGE_SIZE) // MOBA_BLOCK
    n_sel = min(MOBA_TOP_K, n_full)
    first_own = n_full * ppb
    n_own = n_pages - first_own
    k_own_past = k_pool[page_table[:, first_own:]].transpose(0, 2, 1, 3, 4).reshape(b, h, n_own * PAGE_SIZE, d)
    v_own_past = v_pool[page_table[:, first_own:]].transpose(0, 2, 1, 3, 4).reshape(b, h, n_own * PAGE_SIZE, d)
    k_own = jnp.concatenate([k_own_past, k], axis=2)
    v_own = jnp.concatenate([v_own_past, v], axis=2)
    own_mask = jnp.concatenate([jnp.ones((t, n_own * PAGE_SIZE), bool), jnp.tril(jnp.ones((t, t), bool))], axis=-1)
    s_own = jnp.einsum('bhtd,bhkd->bhtk', q, k_own).astype(jnp.float32) * scale
    s_own = jnp.where(own_mask, s_own, -jnp.inf)
    o_dtype = v.dtype
    if n_sel > 0:
        pt_full = page_table[:, :first_own]
        page_mean = jnp.mean(k_pool[pt_full].astype(jnp.float32), axis=3)
        blk_mean = page_mean.reshape(b, n_full, ppb, h, d).mean(axis=2)
        gate = jnp.einsum('bhtd,bnhd->bhtn', q.astype(jnp.float32), blk_mean)
        _, top_idx = lax.top_k(gate, n_sel)
        logical = top_idx[..., None] * ppb + jnp.arange(ppb)
        phys = page_table[jnp.arange(b)[:, None, None, None, None], logical]
        hi = jnp.arange(h)[None, :, None, None, None]
        k_sel = k_pool[phys, hi].reshape(b, h, t, n_sel * MOBA_BLOCK, d)
        v_sel = v_pool[phys, hi].reshape(b, h, t, n_sel * MOBA_BLOCK, d)
        s_sel = jnp.einsum('bhtd,bhtkd->bhtk', q, k_sel).astype(jnp.float32) * scale
        n_key_sel = n_sel * MOBA_BLOCK
        p = jax.nn.softmax(jnp.concatenate([s_sel, s_own], axis=-1), axis=-1).astype(o_dtype)
        return (jnp.einsum('bhtk,bhtkd->bhtd', p[..., :n_key_sel], v_sel)
                + jnp.einsum('bhtk,bhkd->bhtd', p[..., n_key_sel:], v_own))
    p = jax.nn.softmax(s_own, axis=-1).astype(o_dtype)
    return jnp.einsum('bhtk,bhkd->bhtd', p, v_own)


def mixer_layer(x, p_i, pos, s0, attend, norm_w, w_in, w_out, w_pe, w_pg, b_pg):
    rq, rk, rv, rg, mq, mk, mv, mg = input_projection(x, norm_w, w_in)
    ret_o, s_new = retention_branch(rq, rk, rv, pos, s0)
    q, k, v = moba_qkv(mq, mk, mv, pos)
    att_o = attend(q, k, v)
    b, l, _ = x.shape
    att_o = att_o.transpose(0, 2, 1, 3).reshape(b, l, MOBA_WIDTH)
    mixed = jnp.concatenate([ret_o * jax.nn.silu(rg), att_o * jax.nn.silu(mg)], axis=-1)
    x = x + mixed @ w_out
    x = x + jax.nn.sigmoid(x @ w_pg + b_pg) * (p_i @ w_pe)
    return x, k, v, s_new


def to_pages(t):
    b, h, s, d = t.shape
    return t.reshape(b, h, s // PAGE_SIZE, PAGE_SIZE, d).transpose(0, 2, 1, 3, 4)


def setup_inputs(seed: int = 0) -> dict:
    key = jax.random.key(seed)
    ks = jax.random.split(key, 16)
    n_pages = PAST_LEN // PAGE_SIZE
    n_pool = (DEC_BATCH * n_pages * 5) // 4
    nrm = lambda k_, shape: jax.random.normal(k_, shape, dtype=jnp.float32)
    page_table = jax.random.permutation(ks[5], n_pool)[:DEC_BATCH * n_pages].reshape(DEC_BATCH, n_pages).astype(jnp.int32)
    return {
        'x_prompt': nrm(ks[0], (BATCH, SEQ, D_MODEL)),
        'x_sample': nrm(ks[1], (DEC_BATCH, DEC_SEQ, D_MODEL)),
        'cache_k': nrm(ks[2], (DEPTH, n_pool, MOBA_HEADS, PAGE_SIZE, HEAD_DIM)),
        'cache_v': nrm(ks[3], (DEPTH, n_pool, MOBA_HEADS, PAGE_SIZE, HEAD_DIM)),
        'state_ret': nrm(ks[4], (DEPTH, DEC_BATCH, RET_HEADS, HEAD_DIM, HEAD_DIM)),
        'page_table': page_table,
        'p_prompt': nrm(ks[6], (DEPTH, BATCH, SEQ, PLE_DIM)),
        'p_sample': nrm(ks[7], (DEPTH, DEC_BATCH, DEC_SEQ, PLE_DIM)),
        'norm_mix': 1.0 + 0.02 * nrm(ks[8], (DEPTH, D_MODEL)),
        'w_in': nrm(ks[9], (DEPTH, D_MODEL, IN_COLS)) * D_MODEL ** -0.5,
        'w_out': nrm(ks[10], (DEPTH, MIX_WIDTH, D_MODEL)) * MIX_WIDTH ** -0.5,
        'w_pe': nrm(ks[11], (DEPTH, PLE_DIM, D_MODEL)) * PLE_DIM ** -0.5,
        'w_pg': nrm(ks[12], (DEPTH, D_MODEL, D_MODEL)) * D_MODEL ** -0.5,
        'b_pg': 0.01 * nrm(ks[13], (DEPTH, D_MODEL)),
        'norm_final': 1.0 + 0.02 * nrm(ks[14], (D_MODEL,)),
    }


def reference(x_prompt, x_sample, cache_k, cache_v, state_ret, page_table, p_prompt, p_sample,
              norm_mix, w_in, w_out, w_pe, w_pg, b_pg, norm_final):
    s_len = x_prompt.shape[1]
    t_len = x_sample.shape[1]
    past = page_table.shape[1] * PAGE_SIZE
    pos_p = jnp.arange(s_len, dtype=jnp.int32)
    pos_s = past + jnp.arange(t_len, dtype=jnp.int32)
    s0_prompt = jnp.zeros((x_prompt.shape[0], RET_HEADS, HEAD_DIM, HEAD_DIM), x_prompt.dtype)
    hp, hs = x_prompt, x_sample
    k_pr, v_pr, s_pr, k_sa, v_sa, s_sa = [], [], [], [], [], []
    for i in range(DEPTH):
        lw = (norm_mix[i], w_in[i], w_out[i], w_pe[i], w_pg[i], b_pg[i])
        hp, kp_, vp_, sp_ = mixer_layer(hp, p_prompt[i], pos_p, s0_prompt, moba_prompt, *lw)
        attend_s = functools.partial(moba_sample, k_pool=cache_k[i], v_pool=cache_v[i], page_table=page_table)
        hs, ks_, vs_, ss_ = mixer_layer(hs, p_sample[i], pos_s, state_ret[i], attend_s, *lw)
        k_pr.append(to_pages(kp_))
        v_pr.append(to_pages(vp_))
        s_pr.append(sp_)
        k_sa.append(ks_)
        v_sa.append(vs_)
        s_sa.append(ss_)
    y_prompt = rms_norm(hp, norm_final)
    y_sample = rms_norm(hs, norm_final)
    return (y_prompt, y_sample, jnp.stack(k_pr), jnp.stack(v_pr), jnp.stack(s_pr),
            jnp.stack(k_sa), jnp.stack(v_sa), jnp.stack(s_sa))
```

```python
import functools
import math

import jax
import jax.numpy as jnp
from jax import lax
from jax.experimental import pallas as pl
from jax.experimental.pallas import tpu as pltpu

HEAD_DIM = 128
PAGE_SIZE = 128
RET_CHUNK = 128
RET_THETA = 10000.0
MOBA_BLOCK = 256
MOBA_TOP_K = 3
ROPE_THETA = 500000.0
ROT_DIM = HEAD_DIM // 4
NORM_EPS = 1e-6
LANES = 128
SUBLANES = 8
VMEM_LIMIT = 56 * 1024 * 1024

F32 = jnp.float32
BF16 = jnp.bfloat16
NT_DIMS = (((1,), (1,)), ((), ()))

Z_DTYPE = jnp.float32
MIX_DTYPE = jnp.bfloat16


def _cparams(sem):
    return pltpu.CompilerParams(dimension_semantics=sem, vmem_limit_bytes=VMEM_LIMIT)


def _inproj_body(x_ref, nw_ref, w_ref, z_ref, h_ref):
    @pl.when(pl.program_id(1) == 0)
    def _():
        x = x_ref[...]
        y = x * lax.rsqrt(jnp.mean(x * x, axis=-1, keepdims=True) + NORM_EPS)
        h_ref[...] = (y * nw_ref[...]).astype(BF16)

    z_ref[...] = jnp.dot(h_ref[...], w_ref[...], preferred_element_type=F32).astype(z_ref.dtype)


def _inproj(x, norm_w, w_bf, tm, tn):
    m, d = x.shape
    n = w_bf.shape[1]
    return pl.pallas_call(
        _inproj_body,
        out_shape=jax.ShapeDtypeStruct((m, n), Z_DTYPE),
        grid=(m // tm, n // tn),
        in_specs=[
            pl.BlockSpec((tm, d), lambda i, j: (i, 0)),
            pl.BlockSpec((1, d), lambda i, j: (0, 0)),
            pl.BlockSpec((d, tn), lambda i, j: (0, j)),
        ],
        out_specs=pl.BlockSpec((tm, tn), lambda i, j: (i, j)),
        scratch_shapes=[pltpu.VMEM((tm, d), BF16)],
        compiler_params=_cparams(("parallel", "arbitrary")),
        name="inproj",
    )(x, norm_w.reshape(1, d), w_bf)


def _outproj_body(x_ref, ret_ref, att_ref, p_ref, wo_r_ref, wo_a_ref, wpg_ref, bpg_ref, wpe_ref,
                  nf_ref, o_ref, *, final_norm):
    x1 = (x_ref[...]
          + jnp.dot(ret_ref[...], wo_r_ref[...], preferred_element_type=F32)
          + jnp.dot(att_ref[...], wo_a_ref[...], preferred_element_type=F32))
    g = jax.nn.sigmoid(jnp.dot(x1.astype(BF16), wpg_ref[...], preferred_element_type=F32) + bpg_ref[...])
    pe = jnp.dot(p_ref[...].astype(BF16), wpe_ref[...], preferred_element_type=F32)
    x2 = x1 + g * pe
    if final_norm:
        y = x2 * lax.rsqrt(jnp.mean(x2 * x2, axis=-1, keepdims=True) + NORM_EPS)
        x2 = y * nf_ref[...]
    o_ref[...] = x2


def _outproj(x, ret_m, att_m, p, wo_bf, wpg_bf, bpg, wpe_bf, norm_final, tm, final_norm):
    m, d = x.shape
    rw = ret_m.shape[1]
    aw = att_m.shape[1]
    pd = p.shape[1]
    const = lambda i: (0, 0)
    return pl.pallas_call(
        functools.partial(_outproj_body, final_norm=final_norm),
        out_shape=jax.ShapeDtypeStruct((m, d), F32),
        grid=(m // tm,),
        in_specs=[
            pl.BlockSpec((tm, d), lambda i: (i, 0)),
            pl.BlockSpec((tm, rw), lambda i: (i, 0)),
            pl.BlockSpec((tm, aw), lambda i: (i, 0)),
            pl.BlockSpec((tm, pd), lambda i: (i, 0)),
            pl.BlockSpec((rw, d), const),
            pl.BlockSpec((aw, d), lambda i: (rw // aw, 0)),
            pl.BlockSpec((d, d), const),
            pl.BlockSpec((1, d), const),
            pl.BlockSpec((pd, d), const),
            pl.BlockSpec((1, d), const),
        ],
        out_specs=pl.BlockSpec((tm, d), lambda i: (i, 0)),
        compiler_params=_cparams(("parallel",)),
        name="outproj",
    )(x, ret_m, att_m, p, wo_bf, wo_bf, wpg_bf, bpg.reshape(1, d), wpe_bf, norm_final.reshape(1, d))


def _ret_tables(n_heads, c_eff):
    c = RET_CHUNK
    log_g = jnp.log(1.0 - 2.0 ** (-5.0 - jnp.arange(n_heads, dtype=F32)))
    i = jnp.arange(c, dtype=F32)
    rel = i[:, None] - i[None, :]
    din = jnp.where(rel >= 0, jnp.exp(jnp.maximum(rel, 0.0)[None] * log_g[:, None, None]), 0.0)
    qd = jnp.exp((i + 1.0)[None, :] * log_g[:, None])
    kd = jnp.where(i[None, :] < c_eff, jnp.exp((c_eff - 1.0 - i)[None, :] * log_g[:, None]), 0.0)
    cd = jnp.exp(c_eff * log_g)
    bc = lambda t: jnp.broadcast_to(t[..., None], t.shape + (LANES,))
    return din, bc(qd), bc(kd), jnp.broadcast_to(cd[:, None, None], (n_heads, 1, LANES))


def _ret_body(rq_ref, rk_ref, rv_ref, rg_ref, cos_ref, sin_ref, s0_ref, din_ref, qd_ref, kd_ref, cd_ref,
              o_ref, s_ref, *, rows):
    c = RET_CHUNK
    din = din_ref[...]
    qd = qd_ref[...]
    kd = kd_ref[...]
    cd = cd_ref[...]
    s = s0_ref[...]
    n_chunks = max(rows // c, 1)
    live = min(rows, c)

    def tile(ref, n):
        t = ref[pl.ds(n * c, live), :].astype(F32)
        if live < c:
            t = jnp.concatenate([t, jnp.zeros((c - live, HEAD_DIM), F32)], axis=0)
        return t

    for n in range(n_chunks):
        cos = tile(cos_ref, n)
        sin = tile(sin_ref, n)
        q = tile(rq_ref, n)
        k = tile(rk_ref, n)
        v = tile(rv_ref, n)
        g = tile(rg_ref, n)
        q = q * cos + pltpu.roll(q, HEAD_DIM // 2, 1) * sin
        k = (k * cos + pltpu.roll(k, HEAD_DIM // 2, 1) * sin) * (HEAD_DIM ** -0.5)
        vb = v.astype(BF16)
        scores = lax.dot_general(q.astype(BF16), k.astype(BF16), NT_DIMS, preferred_element_type=F32) * din
        o = (jnp.dot(scores.astype(BF16), vb, preferred_element_type=F32)
             + jnp.dot((q * qd).astype(BF16), s.astype(BF16), preferred_element_type=F32))
        s = s * cd + jnp.dot((k * kd).T.astype(BF16), vb, preferred_element_type=F32)
        mu = jnp.mean(o, axis=-1, keepdims=True)
        var = jnp.mean(jnp.square(o - mu), axis=-1, keepdims=True)
        o = (o - mu) * lax.rsqrt(var + NORM_EPS)
        o = o * (g * jax.nn.sigmoid(g))
        o_ref[pl.ds(n * c, live), :] = o[:live].astype(o_ref.dtype)
    s_ref[...] = s


def _retention(z3, cos, sin, s0, tables, n_heads):
    b, rows, _ = z3.shape
    din, qd, kd, cd = tables
    c = RET_CHUNK
    col = lambda off: pl.BlockSpec((None, rows, HEAD_DIM), lambda bi, hi: (bi, 0, off + hi))
    tab = pl.BlockSpec((rows, HEAD_DIM), lambda bi, hi: (0, 0))
    per_head = lambda r: pl.BlockSpec((None, r, LANES), lambda bi, hi: (hi, 0, 0))
    return pl.pallas_call(
        functools.partial(_ret_body, rows=rows),
        out_shape=(jax.ShapeDtypeStruct((b, rows, n_heads * HEAD_DIM), MIX_DTYPE),
                   jax.ShapeDtypeStruct(s0.shape, F32)),
        grid=(b, n_heads),
        in_specs=[col(0), col(n_heads), col(2 * n_heads), col(3 * n_heads), tab, tab,
                  pl.BlockSpec((None, None, HEAD_DIM, HEAD_DIM), lambda bi, hi: (bi, hi, 0, 0)),
                  per_head(c), per_head(c), per_head(c), per_head(1)],
        out_specs=(pl.BlockSpec((None, rows, HEAD_DIM), lambda bi, hi: (bi, 0, hi)),
                   pl.BlockSpec((None, None, HEAD_DIM, HEAD_DIM), lambda bi, hi: (bi, hi, 0, 0))),
        compiler_params=_cparams(("parallel", "parallel")),
        name="retention",
    )(z3, z3, z3, z3, cos, sin, s0, din, qd, kd, cd)


def _rope_tables(pos, rot_dim, theta):
    half = rot_dim // 2
    freqs = theta ** (-jnp.arange(half, dtype=F32) / half)
    ang = pos.astype(F32)[:, None] * freqs[None, :]
    cos, sin = jnp.cos(ang), jnp.sin(ang)
    n = pos.shape[0]
    rest = HEAD_DIM - rot_dim
    zh = jnp.zeros((n, half), F32)
    c = jnp.concatenate([cos, cos, jnp.ones((n, rest), F32)], axis=1)
    sa = jnp.concatenate([-sin, zh, jnp.zeros((n, rest), F32)], axis=1)
    sb = jnp.concatenate([zh, sin, jnp.zeros((n, rest), F32)], axis=1)
    return c, sa, sb


def _moba_rot(x, c, sa, sb):
    half = ROT_DIM // 2
    return x * c + pltpu.roll(x, HEAD_DIM - half, 1) * sa + pltpu.roll(x, half, 1) * sb


def _moba_prompt_body(mq_ref, mk_ref, mv_ref, mg_ref, c_ref, sa_ref, sb_ref, o_ref, nk_ref, nv_ref, *, seq):
    blk = MOBA_BLOCK
    nb = seq // blk
    scale = HEAD_DIM ** -0.5
    c, sa, sb = c_ref[...], sa_ref[...], sb_ref[...]
    q = _moba_rot(mq_ref[...].astype(F32), c, sa, sb)
    k = _moba_rot(mk_ref[...].astype(F32), c, sa, sb)
    v = mv_ref[...].astype(F32)
    nk_ref[...] = k.reshape(seq // PAGE_SIZE, PAGE_SIZE, HEAD_DIM)
    nv_ref[...] = v.reshape(seq // PAGE_SIZE, PAGE_SIZE, HEAD_DIM)

    k_mean = jnp.mean(k.reshape(nb, blk, HEAD_DIM), axis=1)
    gate = lax.dot_general(k_mean, q, NT_DIMS, preferred_element_type=F32,
                           precision=lax.Precision.HIGHEST)
    qb_all = q.astype(BF16)
    kb_all = k.astype(BF16)
    vt_all = v.T.astype(BF16)
    blk_id = lax.broadcasted_iota(jnp.int32, (nb, blk), 0)
    key_i = lax.broadcasted_iota(jnp.int32, (blk, blk), 0)
    qry_i = lax.broadcasted_iota(jnp.int32, (blk, blk), 1)
    causal = key_i <= qry_i

    for qi in range(nb):
        qs = slice(qi * blk, (qi + 1) * blk)
        qb = qb_all[qs]
        if qi > MOBA_TOP_K:
            g = gate[:, qs]
            rank = jnp.zeros((nb, blk), jnp.int32)
            for j in range(qi):
                gj = g[j:j + 1, :]
                ahead = (gj > g) | ((gj == g) & (j < blk_id))
                rank = rank + ahead.astype(jnp.int32)
            bias = jnp.where(rank < MOBA_TOP_K, 0.0, -jnp.inf)
        tiles = []
        for j in range(qi + 1):
            st = lax.dot_general(kb_all[j * blk:(j + 1) * blk], qb, NT_DIMS,
                                 preferred_element_type=F32) * scale
            if j == qi:
                st = jnp.where(causal, st, -jnp.inf)
            elif qi > MOBA_TOP_K:
                st = st + bias[j:j + 1, :]
            tiles.append(st)
        st = tiles[0] if qi == 0 else jnp.concatenate(tiles, axis=0)
        m = jnp.max(st, axis=0, keepdims=True)
        p = jnp.exp(st - m)
        l = jnp.sum(p, axis=0, keepdims=True)
        acc = jnp.dot(vt_all[:, :(qi + 1) * blk], p.astype(BF16), preferred_element_type=F32)
        o = (acc / l).T
        g_ = mg_ref[qs, :].astype(F32)
        o_ref[qs, :] = (o * (g_ * jax.nn.sigmoid(g_))).astype(o_ref.dtype)


def _moba_prompt(z3, tabs, n_ret, n_heads):
    b, seq, _ = z3.shape
    base = 4 * n_ret
    col = lambda off: pl.BlockSpec((None, seq, HEAD_DIM), lambda bi, hi: (bi, 0, base + off + hi))
    tab = pl.BlockSpec((seq, HEAD_DIM), lambda bi, hi: (0, 0))
    n_pages = seq // PAGE_SIZE
    page_spec = pl.BlockSpec((None, n_pages, None, PAGE_SIZE, HEAD_DIM), lambda bi, hi: (bi, 0, hi, 0, 0))
    page_shape = jax.ShapeDtypeStruct((b, n_pages, n_heads, PAGE_SIZE, HEAD_DIM), F32)
    return pl.pallas_call(
        functools.partial(_moba_prompt_body, seq=seq),
        out_shape=(jax.ShapeDtypeStruct((b, seq, n_heads * HEAD_DIM), MIX_DTYPE), page_shape, page_shape),
        grid=(b, n_heads),
        in_specs=[col(0), col(n_heads), col(2 * n_heads), col(3 * n_heads), tab, tab, tab],
        out_specs=(pl.BlockSpec((None, seq, HEAD_DIM), lambda bi, hi: (bi, 0, hi)), page_spec, page_spec),
        compiler_params=_cparams(("parallel", "parallel")),
        name="moba_prompt",
    )(z3, z3, z3, z3, *tabs)


PAGES_PER_STEP = 8


def _blockmean_body(pt_ref, *refs):
    del pt_ref
    page_refs, o_ref = refs[:-1], refs[-1]
    ppb = MOBA_BLOCK // PAGE_SIZE
    for n in range(len(page_refs) // ppb):
        acc = jnp.sum(page_refs[ppb * n][...], axis=1)
        for r in range(1, ppb):
            acc = acc + jnp.sum(page_refs[ppb * n + r][...], axis=1)
        o_ref[n] = acc * (1.0 / MOBA_BLOCK)


def _block_means(cache_k, page_table):
    depth, _, h, _, d = cache_k.shape
    b, n_pages = page_table.shape
    ppb = MOBA_BLOCK // PAGE_SIZE
    n_full = (n_pages * PAGE_SIZE) // MOBA_BLOCK
    g = PAGES_PER_STEP
    assert (n_full * ppb) % g == 0

    def page_spec(j):
        return pl.BlockSpec((None, None, h, PAGE_SIZE, d), lambda l, bi, gi, pt: (l, pt[bi, gi * g + j], 0, 0, 0))

    return pl.pallas_call(
        _blockmean_body,
        out_shape=jax.ShapeDtypeStruct((depth, b, n_full, h, d), F32),
        grid_spec=pltpu.PrefetchScalarGridSpec(
            num_scalar_prefetch=1,
            grid=(depth, b, n_full * ppb // g),
            in_specs=[page_spec(j) for j in range(g)],
            out_specs=pl.BlockSpec((None, None, g // ppb, h, d), lambda l, bi, gi, pt: (l, bi, gi, 0, 0)),
        ),
        compiler_params=_cparams(("parallel", "parallel", "parallel")),
        name="block_means",
    )(page_table, *([cache_k] * g))


def _decode_gate_body(mq_ref, mk_ref, mv_ref, bm_ref, c_ref, sa_ref, sb_ref, q_ref, k_ref, v_ref, idx_ref,
                      *, n_heads, n_full):
    c, sa, sb = c_ref[...], sa_ref[...], sb_ref[...]
    rows = mq_ref.shape[0]
    blk_i = lax.broadcasted_iota(jnp.int32, (n_full, LANES), 0)
    pick_row = lax.broadcasted_iota(jnp.int32, (SUBLANES, LANES), 0)
    for h in range(n_heads):
        cs = slice(h * HEAD_DIM, (h + 1) * HEAD_DIM)
        q = _moba_rot(mq_ref[:, cs].astype(F32), c, sa, sb)
        k = _moba_rot(mk_ref[:, cs].astype(F32), c, sa, sb)
        q_ref[h] = q
        k_ref[h] = k
        v_ref[h] = mv_ref[:, cs].astype(F32)
        q_pad = jnp.concatenate([q, jnp.zeros((LANES - rows, HEAD_DIM), F32)], axis=0)
        bm_h = bm_ref[pl.ds(h, n_full, stride=n_heads), :]
        gate = lax.dot_general(bm_h, q_pad, NT_DIMS, preferred_element_type=F32,
                               precision=lax.Precision.HIGHEST)
        picks = jnp.zeros((SUBLANES, LANES), jnp.int32)
        for r in range(MOBA_TOP_K):
            m = jnp.max(gate, axis=0, keepdims=True)
            pick = jnp.min(jnp.where(gate == m, blk_i, n_full), axis=0, keepdims=True)
            picks = jnp.where(pick_row == r, pick, picks)
            gate = jnp.where(blk_i == pick, -jnp.inf, gate)
        idx_ref[h] = picks


def _decode_gate(z3, bm, tabs, n_ret, n_heads):
    b, rows, _ = z3.shape
    n_full = bm.shape[1]
    bm = bm.reshape(b, n_full * n_heads, HEAD_DIM)
    w = n_heads * HEAD_DIM
    base = 4 * n_ret * HEAD_DIM // w
    col = lambda off: pl.BlockSpec((None, rows, w), lambda bi: (bi, 0, base + off))
    tab = pl.BlockSpec((rows, HEAD_DIM), lambda bi: (0, 0))
    qkv_shape = jax.ShapeDtypeStruct((b, n_heads, rows, HEAD_DIM), F32)
    qkv_spec = pl.BlockSpec((None, n_heads, rows, HEAD_DIM), lambda bi: (bi, 0, 0, 0))
    return pl.pallas_call(
        functools.partial(_decode_gate_body, n_heads=n_heads, n_full=n_full),
        out_shape=(qkv_shape, qkv_shape, qkv_shape,
                   jax.ShapeDtypeStruct((b, n_heads, SUBLANES, LANES), jnp.int32)),
        grid=(b,),
        in_specs=[col(0), col(1), col(2),
                  pl.BlockSpec((None, n_full * n_heads, HEAD_DIM), lambda bi: (bi, 0, 0)),
                  tab, tab, tab],
        out_specs=(qkv_spec, qkv_spec, qkv_spec,
                   pl.BlockSpec((None, n_heads, SUBLANES, LANES), lambda bi: (bi, 0, 0, 0))),
        compiler_params=_cparams(("parallel",)),
        name="decode_gate",
    )(z3, z3, z3, bm, *tabs)


def _decode_attn_body(idx_ref, pt_ref, *refs, n_tok):
    del idx_ref, pt_ref
    ppb = MOBA_BLOCK // PAGE_SIZE
    n_slab = n_tok * MOBA_TOP_K * ppb
    k_slabs, v_slabs = refs[:n_slab], refs[n_slab:2 * n_slab]
    q_ref, k_ref, v_ref, mg_ref, o_ref = refs[2 * n_slab:]
    rows = q_ref.shape[0]
    scale = HEAD_DIM ** -0.5
    qb = q_ref[...].astype(BF16)
    pad = jnp.zeros((LANES - rows, HEAD_DIM), F32)
    k_own = jnp.concatenate([k_ref[...], pad], axis=0).astype(BF16)
    v_own = jnp.concatenate([v_ref[...], pad], axis=0).astype(BF16)
    s_own = lax.dot_general(qb, k_own, NT_DIMS, preferred_element_type=F32) * scale
    tok_i = lax.broadcasted_iota(jnp.int32, (rows, LANES), 0)
    key_i = lax.broadcasted_iota(jnp.int32, (rows, LANES), 1)
    s_own = jnp.where((key_i <= tok_i) & (key_i < n_tok), s_own, -jnp.inf)
    out = jnp.zeros((rows, HEAD_DIM), F32)
    per_tok = MOBA_TOP_K * ppb
    for t in range(n_tok):
        ks = jnp.concatenate([r[...] for r in k_slabs[t * per_tok:(t + 1) * per_tok]], axis=0).astype(BF16)
        vs = jnp.concatenate([r[...] for r in v_slabs[t * per_tok:(t + 1) * per_tok]], axis=0).astype(BF16)
        s_sel = lax.dot_general(qb, ks, NT_DIMS, preferred_element_type=F32) * scale
        m = jnp.maximum(jnp.max(s_sel, axis=-1, keepdims=True), jnp.max(s_own, axis=-1, keepdims=True))
        p_sel = jnp.exp(s_sel - m)
        p_own = jnp.exp(s_own - m)
        l = jnp.sum(p_sel, axis=-1, keepdims=True) + jnp.sum(p_own, axis=-1, keepdims=True)
        o = (jnp.dot(p_sel.astype(BF16), vs, preferred_element_type=F32)
             + jnp.dot(p_own.astype(BF16), v_own, preferred_element_type=F32)) / l
        out = jnp.where(lax.broadcasted_iota(jnp.int32, (rows, HEAD_DIM), 0) == t, o, out)
    g = mg_ref[...].astype(F32)
    o_ref[...] = (out * (g * jax.nn.sigmoid(g))).astype(o_ref.dtype)


def _decode_attn(layer, cache_k, cache_v, idx_flat, page_table, q, k, v, z3, n_ret, n_tok):
    b, n_heads, rows, _ = q.shape
    ppb = MOBA_BLOCK // PAGE_SIZE

    def slab_spec(t, r, j):
        def imap(bi, hi, idx, pt):
            blk = idx[((bi * n_heads + hi) * MOBA_TOP_K + r) * n_tok + t]
            return (layer, pt[bi, blk * ppb + j], hi, 0, 0)
        return pl.BlockSpec((None, None, None, PAGE_SIZE, HEAD_DIM), imap)

    slabs = [slab_spec(t, r, j) for t in range(n_tok) for r in range(MOBA_TOP_K) for j in range(ppb)]
    tok_spec = pl.BlockSpec((None, None, rows, HEAD_DIM), lambda bi, hi, idx, pt: (bi, hi, 0, 0))
    gate_col = 4 * n_ret + 3 * n_heads
    return pl.pallas_call(
        functools.partial(_decode_attn_body, n_tok=n_tok),
        out_shape=jax.ShapeDtypeStruct((b, rows, n_heads * HEAD_DIM), MIX_DTYPE),
        grid_spec=pltpu.PrefetchScalarGridSpec(
            num_scalar_prefetch=2,
            grid=(b, n_heads),
            in_specs=slabs + slabs + [
                tok_spec, tok_spec, tok_spec,
                pl.BlockSpec((None, rows, HEAD_DIM), lambda bi, hi, idx, pt: (bi, 0, gate_col + hi))],
            out_specs=pl.BlockSpec((None, rows, HEAD_DIM), lambda bi, hi, idx, pt: (bi, 0, hi)),
        ),
        compiler_params=_cparams(("parallel", "parallel")),
        name="decode_attn",
    )(idx_flat, page_table, *([cache_k] * len(slabs)), *([cache_v] * len(slabs)), q, k, v, z3)


def _row_tile(m, cap):
    t = min(m, cap)
    while m % t:
        t //= 2
    return t


def kernel(x_prompt, x_sample, cache_k, cache_v, state_ret, page_table, p_prompt, p_sample,
           norm_mix, w_in, w_out, w_pe, w_pg, b_pg, norm_final):
    batch, seq, d_model = x_prompt.shape
    dec_batch, dec_seq, _ = x_sample.shape
    depth = w_in.shape[0]
    in_cols = w_in.shape[2]
    n_ret = in_cols // (8 * HEAD_DIM)
    n_moba = n_ret
    n_pages = page_table.shape[1]
    past = n_pages * PAGE_SIZE
    assert seq % MOBA_BLOCK == 0 and seq % RET_CHUNK == 0
    assert past % MOBA_BLOCK == 0 and past // MOBA_BLOCK >= MOBA_TOP_K
    assert dec_seq <= SUBLANES and RET_CHUNK % dec_seq == 0
    dec_rows = SUBLANES

    w_in_bf = w_in.astype(BF16)
    w_out_bf = w_out.astype(BF16)
    w_pg_bf = w_pg.astype(BF16)
    w_pe_bf = w_pe.astype(BF16)

    pos_p = jnp.arange(seq, dtype=jnp.int32)
    pos_s = past + jnp.arange(dec_rows, dtype=jnp.int32)
    ret_tab_p = _rope_tables(pos_p, HEAD_DIM, RET_THETA)
    ret_tab_s = _rope_tables(pos_s, HEAD_DIM, RET_THETA)
    ret_cs_p = (ret_tab_p[0], ret_tab_p[1] + ret_tab_p[2])
    ret_cs_s = (ret_tab_s[0], ret_tab_s[1] + ret_tab_s[2])
    moba_tab_p = _rope_tables(pos_p, ROT_DIM, ROPE_THETA)
    moba_tab_s = _rope_tables(pos_s, ROT_DIM, ROPE_THETA)
    dec_p = _ret_tables(n_ret, RET_CHUNK)
    dec_s = _ret_tables(n_ret, dec_seq)

    block_means = _block_means(cache_k, page_table)

    hp = x_prompt.reshape(batch * seq, d_model)
    hs = jnp.pad(x_sample, ((0, 0), (0, dec_rows - dec_seq), (0, 0))).reshape(dec_batch * dec_rows, d_model)
    pp = p_prompt.reshape(depth, batch * seq, -1)
    ps = jnp.pad(p_sample, ((0, 0), (0, 0), (0, dec_rows - dec_seq), (0, 0))).reshape(depth, dec_batch * dec_rows, -1)
    s0_prompt = jnp.zeros((batch, n_ret, HEAD_DIM, HEAD_DIM), F32)

    tm_p = _row_tile(batch * seq, 512)
    tm_s = dec_batch * dec_rows
    tn = _row_tile(in_cols, 1024)
    tm_o = _row_tile(batch * seq, 256)

    k_pr, v_pr, s_pr, k_sa, v_sa, s_sa = [], [], [], [], [], []
    for i in range(depth):
        last = i == depth - 1
        z = _inproj(hp, norm_mix[i], w_in_bf[i], tm_p, tn).reshape(batch, seq, in_cols)
        ret_o, s_new = _retention(z, *ret_cs_p, s0_prompt, dec_p, n_ret)
        att_o, nk, nv = _moba_prompt(z, moba_tab_p, n_ret, n_moba)
        hp = _outproj(hp, ret_o.reshape(batch * seq, -1), att_o.reshape(batch * seq, -1), pp[i],
                      w_out_bf[i], w_pg_bf[i], b_pg[i], w_pe_bf[i], norm_final, tm_o, last)
        k_pr.append(nk)
        v_pr.append(nv)
        s_pr.append(s_new)
        zs = _inproj(hs, norm_mix[i], w_in_bf[i], tm_s, tn).reshape(dec_batch, dec_rows, in_cols)
        ret_s, ss_new = _retention(zs, *ret_cs_s, state_ret[i], dec_s, n_ret)
        q, k, v, idx = _decode_gate(zs, block_means[i], moba_tab_s, n_ret, n_moba)
        idx_flat = idx[:, :, :MOBA_TOP_K, :dec_seq].reshape(-1)
        att_s = _decode_attn(i, cache_k, cache_v, idx_flat, page_table, q, k, v, zs, n_ret, dec_seq)
        hs = _outproj(hs, ret_s.reshape(dec_batch * dec_rows, -1), att_s.reshape(dec_batch * dec_rows, -1), ps[i],
                      w_out_bf[i], w_pg_bf[i], b_pg[i], w_pe_bf[i], norm_final, tm_s, last)
        k_sa.append(k[:, :, :dec_seq])
        v_sa.append(v[:, :, :dec_seq])
        s_sa.append(ss_new)

    y_prompt = hp.reshape(batch, seq, d_model)
    y_sample = hs.reshape(dec_batch, dec_rows, d_model)[:, :dec_seq]
    return (y_prompt, y_sample, jnp.stack(k_pr), jnp.stack(v_pr), jnp.stack(s_pr),
            jnp.stack(k_sa), jnp.stack(v_sa), jnp.stack(s_sa))
```

```python
import functools
import math

import jax
import jax.numpy as jnp
from jax import lax
from jax.experimental import pallas as pl
from jax.experimental.pallas import tpu as pltpu

HEAD_DIM = 128
PAGE_SIZE = 128
RET_CHUNK = 128
RET_THETA = 10000.0
MOBA_BLOCK = 256
MOBA_TOP_K = 3
ROPE_THETA = 500000.0
ROT_DIM = HEAD_DIM // 4
NORM_EPS = 1e-6
LANES = 128
SUBLANES = 8
VMEM_LIMIT = 56 * 1024 * 1024

F32 = jnp.float32
BF16 = jnp.bfloat16
NT_DIMS = (((1,), (1,)), ((), ()))
LOG2E = math.log2(math.e)

Z_DTYPE = jnp.bfloat16
MIX_DTYPE = jnp.bfloat16


def _cparams(sem):
    return pltpu.CompilerParams(dimension_semantics=sem, vmem_limit_bytes=VMEM_LIMIT)


def _rms(x, w):
    return x * lax.rsqrt(jnp.mean(x * x, axis=-1, keepdims=True) + NORM_EPS) * w


def _silu(g):
    return g * jax.nn.sigmoid(g)


def _prenorm_body(x_ref, nw_ref, h_ref):
    h_ref[...] = _rms(x_ref[...], nw_ref[...]).astype(h_ref.dtype)


def _prenorm(x, norms, tm):
    m, d = x.shape
    return pl.pallas_call(
        _prenorm_body,
        out_shape=jax.ShapeDtypeStruct((m, d), BF16),
        grid=(m // tm,),
        in_specs=[pl.BlockSpec((tm, d), lambda i: (i, 0)),
                  pl.BlockSpec((None, 1, d), lambda i: (0, 0, 0))],
        out_specs=pl.BlockSpec((tm, d), lambda i: (i, 0)),
        compiler_params=_cparams(("parallel",)),
        name="prenorm",
    )(x, norms)


def _inproj_body(h_ref, hs_ref, w_ref, z_ref, zs_ref, wbf_ref):
    @pl.when(pl.program_id(1) == 0)
    def _():
        wbf_ref[...] = w_ref[...].astype(BF16)
        zs_ref[...] = jnp.dot(hs_ref[...], wbf_ref[...], preferred_element_type=F32).astype(zs_ref.dtype)

    z_ref[...] = jnp.dot(h_ref[...], wbf_ref[...], preferred_element_type=F32).astype(z_ref.dtype)


def _inproj(layer, h, hs, w_in, tm, tn):
    m, d = h.shape
    ms = hs.shape[0]
    n = w_in.shape[2]
    return pl.pallas_call(
        _inproj_body,
        out_shape=(jax.ShapeDtypeStruct((m, n), Z_DTYPE), jax.ShapeDtypeStruct((ms, n), Z_DTYPE)),
        grid=(n // tn, m // tm),
        in_specs=[
            pl.BlockSpec((tm, d), lambda j, i: (i, 0)),
            pl.BlockSpec((ms, d), lambda j, i: (0, 0)),
            pl.BlockSpec((None, d, tn), lambda j, i: (layer, 0, j)),
        ],
        out_specs=(pl.BlockSpec((tm, tn), lambda j, i: (i, j)),
                   pl.BlockSpec((ms, tn), lambda j, i: (0, j))),
        scratch_shapes=[pltpu.VMEM((d, tn), BF16)],
        compiler_params=_cparams(("arbitrary", "arbitrary")),
        name="inproj",
    )(h, hs, w_in)


def _outproj_body(x_ref, ret_ref, att_ref, p_ref, wo_r_ref, wo_a_ref, wpg_ref, bpg_ref, wpe_ref,
                  nw_ref, *out_refs, last):
    x1 = (x_ref[...]
          + jnp.dot(ret_ref[...], wo_r_ref[...], preferred_element_type=F32)
          + jnp.dot(att_ref[...], wo_a_ref[...], preferred_element_type=F32))
    g = jax.nn.sigmoid(jnp.dot(x1.astype(BF16), wpg_ref[...], preferred_element_type=F32) + bpg_ref[...])
    pe = jnp.dot(p_ref[...].astype(BF16), wpe_ref[...], preferred_element_type=F32)
    x2 = x1 + g * pe
    y = _rms(x2, nw_ref[...])
    if last:
        out_refs[0][...] = y
    else:
        out_refs[0][...] = x2
        out_refs[1][...] = y.astype(out_refs[1].dtype)


def _outproj(layer, x, ret_m, att_m, p, wo_bf, wpg_bf, bpg, wpe_bf, norms_next, tm, last):
    m, d = x.shape
    rw = ret_m.shape[1]
    aw = att_m.shape[1]
    pd = p.shape[2]
    row = lambda w: pl.BlockSpec((tm, w), lambda i: (i, 0))
    lay = lambda r: pl.BlockSpec((None, r, d), lambda i: (layer, 0, 0))
    out_shape = [jax.ShapeDtypeStruct((m, d), F32)]
    out_specs = [row(d)]
    if not last:
        out_shape.append(jax.ShapeDtypeStruct((m, d), BF16))
        out_specs.append(row(d))
    return pl.pallas_call(
        functools.partial(_outproj_body, last=last),
        out_shape=tuple(out_shape),
        grid=(m // tm,),
        in_specs=[
            row(d), row(rw), row(aw),
            pl.BlockSpec((None, tm, pd), lambda i: (layer, i, 0)),
            lay(rw),
            pl.BlockSpec((None, aw, d), lambda i: (layer, rw // aw, 0)),
            lay(d), lay(1), lay(pd), lay(1),
        ],
        out_specs=tuple(out_specs),
        compiler_params=_cparams(("parallel",)),
        name="outproj",
    )(x, ret_m, att_m, p, wo_bf, wo_bf, wpg_bf, bpg, wpe_bf, norms_next)


def _ret_tables(n_heads, c_eff):
    c = RET_CHUNK
    log_g = jnp.log(1.0 - 2.0 ** (-5.0 - jnp.arange(n_heads, dtype=F32)))
    i = jnp.arange(c, dtype=F32)
    rel = i[:, None] - i[None, :]
    din = jnp.where(rel >= 0, jnp.exp(jnp.maximum(rel, 0.0)[None] * log_g[:, None, None]), 0.0)
    qd = jnp.exp((i + 1.0)[None, :] * log_g[:, None])
    kd = jnp.where(i[None, :] < c_eff, jnp.exp((c_eff - 1.0 - i)[None, :] * log_g[:, None]), 0.0)
    cd = jnp.exp(c_eff * log_g)
    bc = lambda t: jnp.broadcast_to(t[..., None], t.shape + (LANES,))
    return din, bc(qd), bc(kd), jnp.broadcast_to(cd[:, None, None], (n_heads, 1, LANES))


def _ret_body(rq_ref, rk_ref, rv_ref, rg_ref, cos_ref, sin_ref, s0_ref, din_ref, qd_ref, kd_ref, cd_ref,
              o_ref, s_ref, *, rows, heads):
    c = RET_CHUNK
    n_chunks = max(rows // c, 1)
    live = min(rows, c)

    def tile(ref, n, cs):
        t = ref[pl.ds(n * c, live), cs].astype(F32)
        if live < c:
            t = jnp.concatenate([t, jnp.zeros((c - live, HEAD_DIM), F32)], axis=0)
        return t

    one = slice(0, HEAD_DIM)
    for hh in range(heads):
        cs = slice(hh * HEAD_DIM, (hh + 1) * HEAD_DIM)
        din, qd, kd, cd = din_ref[hh], qd_ref[hh], kd_ref[hh], cd_ref[hh]
        s = s0_ref[hh]
        for n in range(n_chunks):
            cos = tile(cos_ref, n, one)
            sin = tile(sin_ref, n, one)
            q = tile(rq_ref, n, cs)
            k = tile(rk_ref, n, cs)
            v = tile(rv_ref, n, cs)
            g = tile(rg_ref, n, cs)
            q = q * cos + pltpu.roll(q, HEAD_DIM // 2, 1) * sin
            k = (k * cos + pltpu.roll(k, HEAD_DIM // 2, 1) * sin) * (HEAD_DIM ** -0.5)
            vb = v.astype(BF16)
            scores = lax.dot_general(q.astype(BF16), k.astype(BF16), NT_DIMS, preferred_element_type=F32) * din
            o = (jnp.dot(scores.astype(BF16), vb, preferred_element_type=F32)
                 + jnp.dot((q * qd).astype(BF16), s.astype(BF16), preferred_element_type=F32))
            s = s * cd + jnp.dot((k * kd).T.astype(BF16), vb, preferred_element_type=F32)
            mu = jnp.mean(o, axis=-1, keepdims=True)
            var = jnp.mean(jnp.square(o - mu), axis=-1, keepdims=True)
            o = (o - mu) * lax.rsqrt(var + NORM_EPS)
            o = o * _silu(g)
            o_ref[pl.ds(n * c, live), cs] = o[:live].astype(o_ref.dtype)
        s_ref[hh] = s


def _retention(z3, cos, sin, s0, s0_layer, tables, n_heads, heads_per_step):
    b, rows, _ = z3.shape
    din, qd, kd, cd = tables
    c = RET_CHUNK
    hps = heads_per_step
    w = hps * HEAD_DIM
    groups = n_heads // hps
    col = lambda off: pl.BlockSpec((None, rows, w), lambda bi, gi: (bi, 0, off * groups + gi))
    tab = pl.BlockSpec((rows, HEAD_DIM), lambda bi, gi: (0, 0))
    per_head = lambda r: pl.BlockSpec((hps, r, LANES), lambda bi, gi: (gi, 0, 0))
    return pl.pallas_call(
        functools.partial(_ret_body, rows=rows, heads=hps),
        out_shape=(jax.ShapeDtypeStruct((b, rows, n_heads * HEAD_DIM), MIX_DTYPE),
                   jax.ShapeDtypeStruct((b, n_heads, HEAD_DIM, HEAD_DIM), F32)),
        grid=(b, groups),
        in_specs=[col(0), col(1), col(2), col(3), tab, tab,
                  pl.BlockSpec((None, None, hps, HEAD_DIM, HEAD_DIM), lambda bi, gi: (s0_layer, bi, gi, 0, 0)),
                  per_head(c), per_head(c), per_head(c), per_head(1)],
        out_specs=(pl.BlockSpec((None, rows, w), lambda bi, gi: (bi, 0, gi)),
                   pl.BlockSpec((None, hps, HEAD_DIM, HEAD_DIM), lambda bi, gi: (bi, gi, 0, 0))),
        compiler_params=_cparams(("parallel", "parallel")),
        name="retention",
    )(z3, z3, z3, z3, cos, sin, s0, din, qd, kd, cd)


def _rope_tables(pos, rot_dim, theta):
    half = rot_dim // 2
    freqs = theta ** (-jnp.arange(half, dtype=F32) / half)
    ang = pos.astype(F32)[:, None] * freqs[None, :]
    cos, sin = jnp.cos(ang), jnp.sin(ang)
    n = pos.shape[0]
    rest = HEAD_DIM - rot_dim
    zh = jnp.zeros((n, half), F32)
    c = jnp.concatenate([cos, cos, jnp.ones((n, rest), F32)], axis=1)
    sa = jnp.concatenate([-sin, zh, jnp.zeros((n, rest), F32)], axis=1)
    sb = jnp.concatenate([zh, sin, jnp.zeros((n, rest), F32)], axis=1)
    return c, sa, sb


def _moba_rot(x, c, sa, sb):
    half = ROT_DIM // 2
    return x * c + pltpu.roll(x, HEAD_DIM - half, 1) * sa + pltpu.roll(x, half, 1) * sb


def _moba_prompt_body(mq_ref, mk_ref, mv_ref, mg_ref, c_ref, sa_ref, sb_ref, *refs, seq):
    o_ref, nk_ref, nv_ref = refs[-3:]
    blk = MOBA_BLOCK
    nb = seq // blk
    c, sa, sb = c_ref[...], sa_ref[...], sb_ref[...]
    q = _moba_rot(mq_ref[...].astype(F32), c, sa, sb)
    k = _moba_rot(mk_ref[...].astype(F32), c, sa, sb)
    v = mv_ref[...].astype(F32)
    nk_ref[...] = k.reshape(seq // PAGE_SIZE, PAGE_SIZE, HEAD_DIM)
    nv_ref[...] = v.reshape(seq // PAGE_SIZE, PAGE_SIZE, HEAD_DIM)

    k_mean = jnp.mean(k.reshape(nb, blk, HEAD_DIM), axis=1)
    gate = lax.dot_general(k_mean, q, NT_DIMS, preferred_element_type=F32,
                           precision=lax.Precision.HIGHEST)
    qb_all = (q * (HEAD_DIM ** -0.5 * LOG2E)).astype(BF16)
    kb_all = k.astype(BF16)
    vt_all = v.T.astype(BF16)
    blk_id = lax.broadcasted_iota(jnp.int32, (nb, blk), 0)
    key_i = lax.broadcasted_iota(jnp.int32, (blk, blk), 0)
    qry_i = lax.broadcasted_iota(jnp.int32, (blk, blk), 1)
    causal = key_i <= qry_i

    for qi in range(nb):
        qs = slice(qi * blk, (qi + 1) * blk)
        qb = qb_all[qs]
        if qi > MOBA_TOP_K:
            g = gate[:, qs]
            rank = jnp.zeros((nb, blk), jnp.int32)
            for j in range(qi):
                gj = g[j:j + 1, :]
                ahead = (gj > g) | ((gj == g) & (j < blk_id))
                rank = rank + ahead.astype(jnp.int32)
            bias = jnp.where(rank < MOBA_TOP_K, 0.0, -jnp.inf)
        tiles = []
        for j in range(qi + 1):
            st = lax.dot_general(kb_all[j * blk:(j + 1) * blk], qb, NT_DIMS,
                                 preferred_element_type=F32)
            if j == qi:
                st = jnp.where(causal, st, -jnp.inf)
            elif qi > MOBA_TOP_K:
                st = st + bias[j:j + 1, :]
            tiles.append(st)
        st = tiles[0] if qi == 0 else jnp.concatenate(tiles, axis=0)
        m = jnp.max(st, axis=0, keepdims=True)
        p = jnp.exp2(st - m)
        l = jnp.sum(p, axis=0, keepdims=True)
        acc = jnp.dot(vt_all[:, :(qi + 1) * blk], p.astype(BF16), preferred_element_type=F32)
        o = (acc / l).T
        o_ref[qs, :] = (o * _silu(mg_ref[qs, :].astype(F32))).astype(o_ref.dtype)


def _moba_prompt(layer, z3, tabs, n_ret, n_heads, kv_bufs):
    b, seq, _ = z3.shape
    base = 4 * n_ret
    col = lambda off: pl.BlockSpec((None, seq, HEAD_DIM), lambda bi, hi: (bi, 0, base + off + hi))
    tab = pl.BlockSpec((seq, HEAD_DIM), lambda bi, hi: (0, 0))
    n_pages = seq // PAGE_SIZE
    page_spec = pl.BlockSpec((None, None, n_pages, None, PAGE_SIZE, HEAD_DIM),
                             lambda bi, hi: (layer, bi, 0, hi, 0, 0))
    page_shape = jax.ShapeDtypeStruct(kv_bufs[0].shape, F32)
    any_spec = pl.BlockSpec(memory_space=pl.ANY)
    return pl.pallas_call(
        functools.partial(_moba_prompt_body, seq=seq),
        out_shape=(jax.ShapeDtypeStruct((b, seq, n_heads * HEAD_DIM), MIX_DTYPE), page_shape, page_shape),
        grid=(b, n_heads),
        in_specs=[col(0), col(n_heads), col(2 * n_heads), col(3 * n_heads), tab, tab, tab, any_spec, any_spec],
        out_specs=(pl.BlockSpec((None, seq, HEAD_DIM), lambda bi, hi: (bi, 0, hi)), page_spec, page_spec),
        input_output_aliases={7: 1, 8: 2},
        compiler_params=_cparams(("parallel", "parallel")),
        name="moba_prompt",
    )(z3, z3, z3, z3, *tabs, *kv_bufs)


PAGES_PER_STEP = 8


def _blockmean_body(pt_ref, *refs):
    del pt_ref
    page_refs, o_ref = refs[:-1], refs[-1]
    ppb = MOBA_BLOCK // PAGE_SIZE
    for n in range(len(page_refs) // ppb):
        acc = jnp.sum(page_refs[ppb * n][...], axis=1)
        for r in range(1, ppb):
            acc = acc + jnp.sum(page_refs[ppb * n + r][...], axis=1)
        o_ref[n] = acc * (1.0 / MOBA_BLOCK)


def _block_means(cache_k, page_table):
    depth, _, h, _, d = cache_k.shape
    b, n_pages = page_table.shape
    ppb = MOBA_BLOCK // PAGE_SIZE
    n_full = (n_pages * PAGE_SIZE) // MOBA_BLOCK
    g = PAGES_PER_STEP
    assert (n_full * ppb) % g == 0

    def page_spec(j):
        return pl.BlockSpec((None, None, h, PAGE_SIZE, d), lambda l, bi, gi, pt: (l, pt[bi, gi * g + j], 0, 0, 0))

    return pl.pallas_call(
        _blockmean_body,
        out_shape=jax.ShapeDtypeStruct((depth, b, n_full, h, d), F32),
        grid_spec=pltpu.PrefetchScalarGridSpec(
            num_scalar_prefetch=1,
            grid=(depth, b, n_full * ppb // g),
            in_specs=[page_spec(j) for j in range(g)],
            out_specs=pl.BlockSpec((None, None, g // ppb, h, d), lambda l, bi, gi, pt: (l, bi, gi, 0, 0)),
        ),
        compiler_params=_cparams(("parallel", "parallel", "parallel")),
        name="block_means",
    )(page_table, *([cache_k] * g))


def _decode_gate_body(mq_ref, mk_ref, mv_ref, bm_ref, c_ref, sa_ref, sb_ref, q_ref, k_ref, v_ref, idx_ref,
                      *, n_heads, n_full):
    c, sa, sb = c_ref[...], sa_ref[...], sb_ref[...]
    rows = mq_ref.shape[0]
    blk_i = lax.broadcasted_iota(jnp.int32, (n_full, LANES), 0)
    pick_row = lax.broadcasted_iota(jnp.int32, (SUBLANES, LANES), 0)
    for h in range(n_heads):
        cs = slice(h * HEAD_DIM, (h + 1) * HEAD_DIM)
        q = _moba_rot(mq_ref[:, cs].astype(F32), c, sa, sb)
        k = _moba_rot(mk_ref[:, cs].astype(F32), c, sa, sb)
        q_ref[h] = q
        k_ref[h] = k
        v_ref[h] = mv_ref[:, cs].astype(F32)
        q_pad = jnp.concatenate([q, jnp.zeros((LANES - rows, HEAD_DIM), F32)], axis=0)
        bm_h = bm_ref[pl.ds(h, n_full, stride=n_heads), :]
        gate = lax.dot_general(bm_h, q_pad, NT_DIMS, preferred_element_type=F32,
                               precision=lax.Precision.HIGHEST)
        picks = jnp.zeros((SUBLANES, LANES), jnp.int32)
        for r in range(MOBA_TOP_K):
            m = jnp.max(gate, axis=0, keepdims=True)
            pick = jnp.min(jnp.where(gate == m, blk_i, n_full), axis=0, keepdims=True)
            picks = jnp.where(pick_row == r, pick, picks)
            gate = jnp.where(blk_i == pick, -jnp.inf, gate)
        idx_ref[h] = picks


def _decode_gate(layer, z3, bm, tabs, n_ret, n_heads):
    b, rows, _ = z3.shape
    n_full = bm.shape[2] // n_heads
    w = n_heads * HEAD_DIM
    base = 4 * n_ret * HEAD_DIM // w
    col = lambda off: pl.BlockSpec((None, rows, w), lambda bi: (bi, 0, base + off))
    tab = pl.BlockSpec((rows, HEAD_DIM), lambda bi: (0, 0))
    qkv_shape = jax.ShapeDtypeStruct((b, n_heads, rows, HEAD_DIM), F32)
    qkv_spec = pl.BlockSpec((None, n_heads, rows, HEAD_DIM), lambda bi: (bi, 0, 0, 0))
    return pl.pallas_call(
        functools.partial(_decode_gate_body, n_heads=n_heads, n_full=n_full),
        out_shape=(qkv_shape, qkv_shape, qkv_shape,
                   jax.ShapeDtypeStruct((b, n_heads, SUBLANES, LANES), jnp.int32)),
        grid=(b,),
        in_specs=[col(0), col(1), col(2),
                  pl.BlockSpec((None, None, n_full * n_heads, HEAD_DIM), lambda bi: (layer, bi, 0, 0)),
                  tab, tab, tab],
        out_specs=(qkv_spec, qkv_spec, qkv_spec,
                   pl.BlockSpec((None, n_heads, SUBLANES, LANES), lambda bi: (bi, 0, 0, 0))),
        compiler_params=_cparams(("parallel",)),
        name="decode_gate",
    )(z3, z3, z3, bm, *tabs)


def _decode_attn_body(idx_ref, pt_ref, *refs, n_tok):
    del idx_ref, pt_ref
    ppb = MOBA_BLOCK // PAGE_SIZE
    per_tok = MOBA_TOP_K * ppb
    n_slab = n_tok * per_tok
    k_slabs, v_slabs = refs[:n_slab], refs[n_slab:2 * n_slab]
    q_ref, k_ref, v_ref, mg_ref, o_ref = refs[2 * n_slab:]
    rows = q_ref.shape[0]
    tok_keys = per_tok * PAGE_SIZE
    qb = (q_ref[...] * (HEAD_DIM ** -0.5)).astype(BF16)
    ks = jnp.concatenate([r[...].astype(BF16) for r in k_slabs], axis=0)
    vs = jnp.concatenate([r[...].astype(BF16) for r in v_slabs], axis=0)
    s_sel = lax.dot_general(qb, ks, NT_DIMS, preferred_element_type=F32)
    row_lo = lax.broadcasted_iota(jnp.int32, s_sel.shape, 0) * tok_keys
    col_i = lax.broadcasted_iota(jnp.int32, s_sel.shape, 1)
    s_sel = jnp.where((col_i >= row_lo) & (col_i < row_lo + tok_keys), s_sel, -jnp.inf)
    pad = jnp.zeros((LANES - rows, HEAD_DIM), F32)
    k_own = jnp.concatenate([k_ref[...], pad], axis=0).astype(BF16)
    v_own = jnp.concatenate([v_ref[...], pad], axis=0).astype(BF16)
    s_own = lax.dot_general(qb, k_own, NT_DIMS, preferred_element_type=F32)
    tok_i = lax.broadcasted_iota(jnp.int32, (rows, LANES), 0)
    key_i = lax.broadcasted_iota(jnp.int32, (rows, LANES), 1)
    s_own = jnp.where((key_i <= tok_i) & (key_i < n_tok), s_own, -jnp.inf)
    m = jnp.maximum(jnp.max(s_sel, axis=-1, keepdims=True), jnp.max(s_own, axis=-1, keepdims=True))
    p_sel = jnp.exp(s_sel - m)
    p_own = jnp.exp(s_own - m)
    l = jnp.sum(p_sel, axis=-1, keepdims=True) + jnp.sum(p_own, axis=-1, keepdims=True)
    o = (jnp.dot(p_sel.astype(BF16), vs, preferred_element_type=F32)
         + jnp.dot(p_own.astype(BF16), v_own, preferred_element_type=F32)) / l
    o_ref[...] = (o * _silu(mg_ref[...].astype(F32))).astype(o_ref.dtype)


def _decode_attn(layer, cache_k, cache_v, idx_flat, page_table, q, k, v, z3, n_ret, n_tok):
    b, n_heads, rows, _ = q.shape
    ppb = MOBA_BLOCK // PAGE_SIZE

    def slab_spec(t, r, j):
        def imap(bi, hi, idx, pt):
            blk = idx[((bi * n_heads + hi) * MOBA_TOP_K + r) * n_tok + t]
            return (layer, pt[bi, blk * ppb + j], hi, 0, 0)
        return pl.BlockSpec((None, None, None, PAGE_SIZE, HEAD_DIM), imap)

    slabs = [slab_spec(t, r, j) for t in range(n_tok) for r in range(MOBA_TOP_K) for j in range(ppb)]
    tok_spec = pl.BlockSpec((None, None, rows, HEAD_DIM), lambda bi, hi, idx, pt: (bi, hi, 0, 0))
    gate_col = 4 * n_ret + 3 * n_heads
    return pl.pallas_call(
        functools.partial(_decode_attn_body, n_tok=n_tok),
        out_shape=jax.ShapeDtypeStruct((b, rows, n_heads * HEAD_DIM), MIX_DTYPE),
        grid_spec=pltpu.PrefetchScalarGridSpec(
            num_scalar_prefetch=2,
            grid=(b, n_heads),
            in_specs=slabs + slabs + [
                tok_spec, tok_spec, tok_spec,
                pl.BlockSpec((None, rows, HEAD_DIM), lambda bi, hi, idx, pt: (bi, 0, gate_col + hi))],
            out_specs=pl.BlockSpec((None, rows, HEAD_DIM), lambda bi, hi, idx, pt: (bi, 0, hi)),
        ),
        compiler_params=_cparams(("parallel", "parallel")),
        name="decode_attn",
    )(idx_flat, page_table, *([cache_k] * len(slabs)), *([cache_v] * len(slabs)), q, k, v, z3)


def _row_tile(m, cap):
    t = min(m, cap)
    while m % t:
        t //= 2
    return t


def kernel(x_prompt, x_sample, cache_k, cache_v, state_ret, page_table, p_prompt, p_sample,
           norm_mix, w_in, w_out, w_pe, w_pg, b_pg, norm_final):
    batch, seq, d_model = x_prompt.shape
    dec_batch, dec_seq, _ = x_sample.shape
    depth = w_in.shape[0]
    in_cols = w_in.shape[2]
    n_ret = in_cols // (8 * HEAD_DIM)
    n_moba = n_ret
    n_pages = page_table.shape[1]
    past = n_pages * PAGE_SIZE
    assert seq % MOBA_BLOCK == 0 and seq % RET_CHUNK == 0
    assert past % MOBA_BLOCK == 0 and past // MOBA_BLOCK >= MOBA_TOP_K
    assert dec_seq <= SUBLANES and RET_CHUNK % dec_seq == 0
    dec_rows = SUBLANES
    m_p = batch * seq
    m_s = dec_batch * dec_rows

    w_out_bf = w_out.astype(BF16)
    w_pg_bf = w_pg.astype(BF16)
    w_pe_bf = w_pe.astype(BF16)
    norms_in = norm_mix.reshape(depth, 1, d_model)
    norms_next = jnp.concatenate([norm_mix[1:], norm_final[None]], axis=0).reshape(depth, 1, d_model)
    bpg = b_pg.reshape(depth, 1, d_model)

    pos_p = jnp.arange(seq, dtype=jnp.int32)
    pos_s = past + jnp.arange(dec_rows, dtype=jnp.int32)
    ret_tab_p = _rope_tables(pos_p, HEAD_DIM, RET_THETA)
    ret_tab_s = _rope_tables(pos_s, HEAD_DIM, RET_THETA)
    ret_cs_p = (ret_tab_p[0], ret_tab_p[1] + ret_tab_p[2])
    ret_cs_s = (ret_tab_s[0], ret_tab_s[1] + ret_tab_s[2])
    moba_tab_p = _rope_tables(pos_p, ROT_DIM, ROPE_THETA)
    moba_tab_s = _rope_tables(pos_s, ROT_DIM, ROPE_THETA)
    dec_p = _ret_tables(n_ret, RET_CHUNK)
    dec_s = _ret_tables(n_ret, dec_seq)

    block_means = _block_means(cache_k, page_table)
    block_means = block_means.reshape(depth, dec_batch, -1, HEAD_DIM)

    xp = x_prompt.reshape(m_p, d_model)
    xs = jnp.pad(x_sample, ((0, 0), (0, dec_rows - dec_seq), (0, 0))).reshape(m_s, d_model)
    pp = p_prompt.reshape(depth, m_p, -1)
    ps = jnp.pad(p_sample, ((0, 0), (0, 0), (0, dec_rows - dec_seq), (0, 0))).reshape(depth, m_s, -1)
    s0_prompt = jnp.zeros((1, batch, n_ret, HEAD_DIM, HEAD_DIM), F32)

    tm_in = _row_tile(m_p, 1024)
    tn = _row_tile(in_cols, 1024)
    tm_o = _row_tile(m_p, 256)

    kv_shape = (depth, batch, seq // PAGE_SIZE, n_moba, PAGE_SIZE, HEAD_DIM)
    kv_bufs = (jnp.zeros(kv_shape, F32), jnp.zeros(kv_shape, F32))

    hp = _prenorm(xp, norms_in, tm_o)
    hs = _prenorm(xs, norms_in, m_s)
    s_pr, k_sa, v_sa, s_sa = [], [], [], []
    for i in range(depth):
        last = i == depth - 1
        z, zs = _inproj(i, hp, hs, w_in, tm_in, tn)
        z = z.reshape(batch, seq, in_cols)
        zs = zs.reshape(dec_batch, dec_rows, in_cols)
        ret_o, s_new = _retention(z, *ret_cs_p, s0_prompt, 0, dec_p, n_ret, 1)
        att_o, *kv_bufs = _moba_prompt(i, z, moba_tab_p, n_ret, n_moba, kv_bufs)
        outs = _outproj(i, xp, ret_o.reshape(m_p, -1), att_o.reshape(m_p, -1), pp,
                        w_out_bf, w_pg_bf, bpg, w_pe_bf, norms_next, tm_o, last)
        xp = outs[0]
        hp = None if last else outs[1]
        s_pr.append(s_new)
        ret_s, ss_new = _retention(zs, *ret_cs_s, state_ret, i, dec_s, n_ret, n_ret)
        q, k, v, idx = _decode_gate(i, zs, block_means, moba_tab_s, n_ret, n_moba)
        idx_flat = idx[:, :, :MOBA_TOP_K, :dec_seq].reshape(-1)
        att_s = _decode_attn(i, cache_k, cache_v, idx_flat, page_table, q, k, v, zs, n_ret, dec_seq)
        outs = _outproj(i, xs, ret_s.reshape(m_s, -1), att_s.reshape(m_s, -1), ps,
                        w_out_bf, w_pg_bf, bpg, w_pe_bf, norms_next, m_s, last)
        xs = outs[0]
        hs = None if last else outs[1]
        k_sa.append(k[:, :, :dec_seq])
        v_sa.append(v[:, :, :dec_seq])
        s_sa.append(ss_new)

    y_prompt = xp.reshape(batch, seq, d_model)
    y_sample = xs.reshape(dec_batch, dec_rows, d_model)[:, :dec_seq]
    return (y_prompt, y_sample, kv_bufs[0], kv_bufs[1], jnp.stack(s_pr),
            jnp.stack(k_sa), jnp.stack(v_sa), jnp.stack(s_sa))
```

```python
import functools
import math

import jax
import jax.numpy as jnp
from jax import lax
from jax.experimental import pallas as pl
from jax.experimental.pallas import tpu as pltpu

HEAD_DIM = 128
PAGE_SIZE = 128
RET_CHUNK = 128
RET_THETA = 10000.0
MOBA_BLOCK = 256
MOBA_TOP_K = 3
ROPE_THETA = 500000.0
ROT_DIM = HEAD_DIM // 4
NORM_EPS = 1e-6
LANES = 128
SUBLANES = 8
VMEM_LIMIT = 56 * 1024 * 1024

F32 = jnp.float32
BF16 = jnp.bfloat16
NT_DIMS = (((1,), (1,)), ((), ()))
LOG2E = math.log2(math.e)

Z_DTYPE = jnp.bfloat16
MIX_DTYPE = jnp.bfloat16


def _cparams(sem):
    return pltpu.CompilerParams(dimension_semantics=sem, vmem_limit_bytes=VMEM_LIMIT)


def _rms(x, w):
    return x * lax.rsqrt(jnp.mean(x * x, axis=-1, keepdims=True) + NORM_EPS) * w


def _silu(g):
    return g * jax.nn.sigmoid(g)


def _prenorm_body(x_ref, nw_ref, h_ref):
    h_ref[...] = _rms(x_ref[...], nw_ref[...]).astype(h_ref.dtype)


def _prenorm(x, norms, tm):
    m, d = x.shape
    return pl.pallas_call(
        _prenorm_body,
        out_shape=jax.ShapeDtypeStruct((m, d), BF16),
        grid=(m // tm,),
        in_specs=[pl.BlockSpec((tm, d), lambda i: (i, 0)),
                  pl.BlockSpec((None, 1, d), lambda i: (0, 0, 0))],
        out_specs=pl.BlockSpec((tm, d), lambda i: (i, 0)),
        compiler_params=_cparams(("parallel",)),
        name="prenorm",
    )(x, norms)


def _sum_pages(page_refs, o_ref):
    ppb = MOBA_BLOCK // PAGE_SIZE
    for n in range(len(page_refs) // ppb):
        acc = jnp.sum(page_refs[ppb * n][...], axis=1)
        for r in range(1, ppb):
            acc = acc + jnp.sum(page_refs[ppb * n + r][...], axis=1)
        o_ref[n] = acc * (1.0 / MOBA_BLOCK)


def _inproj_body(pt_ref, h_ref, hs_ref, w_ref, *refs):
    del pt_ref
    n_pages = len(refs) - 4
    page_refs = refs[:n_pages]
    z_ref, zs_ref, bm_ref, wbf_ref = refs[n_pages:]

    @pl.when(pl.program_id(1) == 0)
    def _():
        wbf_ref[...] = w_ref[...].astype(BF16)
        zs_ref[...] = jnp.dot(hs_ref[...], wbf_ref[...], preferred_element_type=F32).astype(zs_ref.dtype)

    z_ref[...] = jnp.dot(h_ref[...], wbf_ref[...], preferred_element_type=F32).astype(z_ref.dtype)
    _sum_pages(page_refs, bm_ref)


def _inproj(layer, h, hs, w_in, cache_k, page_table, tm, tn, g):
    m, d = h.shape
    ms = hs.shape[0]
    n = w_in.shape[2]
    _, _, heads, _, hd = cache_k.shape
    b, n_pages = page_table.shape
    ppb = MOBA_BLOCK // PAGE_SIZE
    n_i = m // tm
    steps_per_b = n_pages // g
    assert (n // tn) * n_i * g == b * n_pages and n_pages % g == 0 and g % ppb == 0

    def page_spec(r):
        def imap(j, i, pt):
            s = j * n_i + i
            return (layer, pt[s // steps_per_b, (s % steps_per_b) * g + r], 0, 0, 0)
        return pl.BlockSpec((None, None, heads, PAGE_SIZE, hd), imap)

    def bm_map(j, i, pt):
        s = j * n_i + i
        return (s // steps_per_b, s % steps_per_b, 0, 0)

    return pl.pallas_call(
        _inproj_body,
        out_shape=(jax.ShapeDtypeStruct((m, n), Z_DTYPE), jax.ShapeDtypeStruct((ms, n), Z_DTYPE),
                   jax.ShapeDtypeStruct((b, n_pages // ppb, heads, hd), F32)),
        grid_spec=pltpu.PrefetchScalarGridSpec(
            num_scalar_prefetch=1,
            grid=(n // tn, n_i),
            in_specs=[
                pl.BlockSpec((tm, d), lambda j, i, pt: (i, 0)),
                pl.BlockSpec((ms, d), lambda j, i, pt: (0, 0)),
                pl.BlockSpec((None, d, tn), lambda j, i, pt: (layer, 0, j)),
            ] + [page_spec(r) for r in range(g)],
            out_specs=(pl.BlockSpec((tm, tn), lambda j, i, pt: (i, j)),
                       pl.BlockSpec((ms, tn), lambda j, i, pt: (0, j)),
                       pl.BlockSpec((None, g // ppb, heads, hd), bm_map)),
            scratch_shapes=[pltpu.VMEM((d, tn), BF16)],
        ),
        compiler_params=_cparams(("arbitrary", "arbitrary")),
        name="inproj",
    )(page_table, h, hs, w_in, *([cache_k] * g))


def _outproj_body(x_ref, ret_ref, att_ref, p_ref, wo_r_ref, wo_a_ref, wpg_ref, bpg_ref, wpe_ref,
                  nw_ref, *out_refs, last):
    x1 = (x_ref[...]
          + jnp.dot(ret_ref[...], wo_r_ref[...], preferred_element_type=F32)
          + jnp.dot(att_ref[...], wo_a_ref[...], preferred_element_type=F32))
    g = jax.nn.sigmoid(jnp.dot(x1.astype(BF16), wpg_ref[...], preferred_element_type=F32) + bpg_ref[...])
    pe = jnp.dot(p_ref[...].astype(BF16), wpe_ref[...], preferred_element_type=F32)
    x2 = x1 + g * pe
    y = _rms(x2, nw_ref[...])
    if last:
        out_refs[0][...] = y
    else:
        out_refs[0][...] = x2
        out_refs[1][...] = y.astype(out_refs[1].dtype)


def _outproj(layer, x, ret_m, att_m, p, wo_bf, wpg_bf, bpg, wpe_bf, norms_next, tm, last):
    m, d = x.shape
    rw = ret_m.shape[1]
    aw = att_m.shape[1]
    pd = p.shape[2]
    row = lambda w: pl.BlockSpec((tm, w), lambda i: (i, 0))
    lay = lambda r: pl.BlockSpec((None, r, d), lambda i: (layer, 0, 0))
    out_shape = [jax.ShapeDtypeStruct((m, d), F32)]
    out_specs = [row(d)]
    if not last:
        out_shape.append(jax.ShapeDtypeStruct((m, d), BF16))
        out_specs.append(row(d))
    return pl.pallas_call(
        functools.partial(_outproj_body, last=last),
        out_shape=tuple(out_shape),
        grid=(m // tm,),
        in_specs=[
            row(d), row(rw), row(aw),
            pl.BlockSpec((None, tm, pd), lambda i: (layer, i, 0)),
            lay(rw),
            pl.BlockSpec((None, aw, d), lambda i: (layer, rw // aw, 0)),
            lay(d), lay(1), lay(pd), lay(1),
        ],
        out_specs=tuple(out_specs),
        compiler_params=_cparams(("parallel",)),
        name="outproj",
    )(x, ret_m, att_m, p, wo_bf, wo_bf, wpg_bf, bpg, wpe_bf, norms_next)


def _ret_tables(n_heads, c_eff):
    c = RET_CHUNK
    log_g = jnp.log(1.0 - 2.0 ** (-5.0 - jnp.arange(n_heads, dtype=F32)))
    i = jnp.arange(c, dtype=F32)
    rel = i[:, None] - i[None, :]
    din = jnp.where(rel >= 0, jnp.exp(jnp.maximum(rel, 0.0)[None] * log_g[:, None, None]), 0.0)
    qd = jnp.exp((i + 1.0)[None, :] * log_g[:, None])
    kd = jnp.where(i[None, :] < c_eff, jnp.exp((c_eff - 1.0 - i)[None, :] * log_g[:, None]), 0.0)
    cd = jnp.exp(c_eff * log_g)
    bc = lambda t: jnp.broadcast_to(t[..., None], t.shape + (LANES,))
    return din, bc(qd), bc(kd), jnp.broadcast_to(cd[:, None, None], (n_heads, 1, LANES))


def _ret_body(rq_ref, rk_ref, rv_ref, rg_ref, cos_ref, sin_ref, s0_ref, din_ref, qd_ref, kd_ref, cd_ref,
              o_ref, s_ref, *, rows, heads):
    c = RET_CHUNK
    n_chunks = max(rows // c, 1)
    live = min(rows, c)

    def tile(ref, n, cs):
        t = ref[pl.ds(n * c, live), cs].astype(F32)
        if live < c:
            t = jnp.concatenate([t, jnp.zeros((c - live, HEAD_DIM), F32)], axis=0)
        return t

    one = slice(0, HEAD_DIM)
    for hh in range(heads):
        cs = slice(hh * HEAD_DIM, (hh + 1) * HEAD_DIM)
        din, qd, kd, cd = din_ref[hh], qd_ref[hh], kd_ref[hh], cd_ref[hh]
        s = s0_ref[hh]
        for n in range(n_chunks):
            cos = tile(cos_ref, n, one)
            sin = tile(sin_ref, n, one)
            q = tile(rq_ref, n, cs)
            k = tile(rk_ref, n, cs)
            v = tile(rv_ref, n, cs)
            g = tile(rg_ref, n, cs)
            q = q * cos + pltpu.roll(q, HEAD_DIM // 2, 1) * sin
            k = (k * cos + pltpu.roll(k, HEAD_DIM // 2, 1) * sin) * (HEAD_DIM ** -0.5)
            vb = v.astype(BF16)
            scores = lax.dot_general(q.astype(BF16), k.astype(BF16), NT_DIMS, preferred_element_type=F32) * din
            o = (jnp.dot(scores.astype(BF16), vb, preferred_element_type=F32)
                 + jnp.dot((q * qd).astype(BF16), s.astype(BF16), preferred_element_type=F32))
            s = s * cd + jnp.dot((k * kd).T.astype(BF16), vb, preferred_element_type=F32)
            mu = jnp.mean(o, axis=-1, keepdims=True)
            var = jnp.mean(jnp.square(o - mu), axis=-1, keepdims=True)
            o = (o - mu) * lax.rsqrt(var + NORM_EPS)
            o = o * _silu(g)
            o_ref[pl.ds(n * c, live), cs] = o[:live].astype(o_ref.dtype)
        s_ref[hh] = s


def _retention(z3, cos, sin, s0, s0_layer, tables, n_heads, heads_per_step):
    b, rows, _ = z3.shape
    din, qd, kd, cd = tables
    c = RET_CHUNK
    hps = heads_per_step
    w = hps * HEAD_DIM
    groups = n_heads // hps
    col = lambda off: pl.BlockSpec((None, rows, w), lambda bi, gi: (bi, 0, off * groups + gi))
    tab = pl.BlockSpec((rows, HEAD_DIM), lambda bi, gi: (0, 0))
    per_head = lambda r: pl.BlockSpec((hps, r, LANES), lambda bi, gi: (gi, 0, 0))
    return pl.pallas_call(
        functools.partial(_ret_body, rows=rows, heads=hps),
        out_shape=(jax.ShapeDtypeStruct((b, rows, n_heads * HEAD_DIM), MIX_DTYPE),
                   jax.ShapeDtypeStruct((b, n_heads, HEAD_DIM, HEAD_DIM), F32)),
        grid=(b, groups),
        in_specs=[col(0), col(1), col(2), col(3), tab, tab,
                  pl.BlockSpec((None, None, hps, HEAD_DIM, HEAD_DIM), lambda bi, gi: (s0_layer, bi, gi, 0, 0)),
                  per_head(c), per_head(c), per_head(c), per_head(1)],
        out_specs=(pl.BlockSpec((None, rows, w), lambda bi, gi: (bi, 0, gi)),
                   pl.BlockSpec((None, hps, HEAD_DIM, HEAD_DIM), lambda bi, gi: (bi, gi, 0, 0))),
        compiler_params=_cparams(("parallel", "parallel")),
        name="retention",
    )(z3, z3, z3, z3, cos, sin, s0, din, qd, kd, cd)


def _rope_tables(pos, rot_dim, theta):
    half = rot_dim // 2
    freqs = theta ** (-jnp.arange(half, dtype=F32) / half)
    ang = pos.astype(F32)[:, None] * freqs[None, :]
    cos, sin = jnp.cos(ang), jnp.sin(ang)
    n = pos.shape[0]
    rest = HEAD_DIM - rot_dim
    zh = jnp.zeros((n, half), F32)
    c = jnp.concatenate([cos, cos, jnp.ones((n, rest), F32)], axis=1)
    sa = jnp.concatenate([-sin, zh, jnp.zeros((n, rest), F32)], axis=1)
    sb = jnp.concatenate([zh, sin, jnp.zeros((n, rest), F32)], axis=1)
    return c, sa, sb


def _moba_rot(x, c, sa, sb):
    half = ROT_DIM // 2
    return x * c + pltpu.roll(x, HEAD_DIM - half, 1) * sa + pltpu.roll(x, half, 1) * sb


def _moba_prompt_body(mq_ref, mk_ref, mv_ref, mg_ref, c_ref, sa_ref, sb_ref, *refs, seq):
    o_ref, nk_ref, nv_ref = refs[-3:]
    blk = MOBA_BLOCK
    nb = seq // blk
    c, sa, sb = c_ref[...], sa_ref[...], sb_ref[...]
    q = _moba_rot(mq_ref[...].astype(F32), c, sa, sb)
    k = _moba_rot(mk_ref[...].astype(F32), c, sa, sb)
    v = mv_ref[...].astype(F32)
    nk_ref[...] = k.reshape(seq // PAGE_SIZE, PAGE_SIZE, HEAD_DIM)
    nv_ref[...] = v.reshape(seq // PAGE_SIZE, PAGE_SIZE, HEAD_DIM)

    k_mean = jnp.mean(k.reshape(nb, blk, HEAD_DIM), axis=1)
    gate = lax.dot_general(k_mean, q, NT_DIMS, preferred_element_type=F32,
                           precision=lax.Precision.HIGHEST)
    qb_all = (q * (HEAD_DIM ** -0.5 * LOG2E)).astype(BF16)
    kb_all = k.astype(BF16)
    vt_all = v.T.astype(BF16)
    blk_id = lax.broadcasted_iota(jnp.int32, (nb, blk), 0)
    key_i = lax.broadcasted_iota(jnp.int32, (blk, blk), 0)
    qry_i = lax.broadcasted_iota(jnp.int32, (blk, blk), 1)
    causal = key_i <= qry_i

    for qi in range(nb):
        qs = slice(qi * blk, (qi + 1) * blk)
        qb = qb_all[qs]
        if qi > MOBA_TOP_K:
            g = gate[:, qs]
            rank = jnp.zeros((nb, blk), jnp.int32)
            for j in range(qi):
                gj = g[j:j + 1, :]
                ahead = (gj > g) | ((gj == g) & (j < blk_id))
                rank = rank + ahead.astype(jnp.int32)
            bias = jnp.where(rank < MOBA_TOP_K, 0.0, -jnp.inf)
        tiles = []
        for j in range(qi + 1):
            st = lax.dot_general(kb_all[j * blk:(j + 1) * blk], qb, NT_DIMS,
                                 preferred_element_type=F32)
            if j == qi:
                st = jnp.where(causal, st, -jnp.inf)
            elif qi > MOBA_TOP_K:
                st = st + bias[j:j + 1, :]
            tiles.append(st)
        st = tiles[0] if qi == 0 else jnp.concatenate(tiles, axis=0)
        m = jnp.max(st, axis=0, keepdims=True)
        p = jnp.exp2(st - m)
        l = jnp.sum(p, axis=0, keepdims=True)
        acc = jnp.dot(vt_all[:, :(qi + 1) * blk], p.astype(BF16), preferred_element_type=F32)
        o = (acc / l).T
        o_ref[qs, :] = (o * _silu(mg_ref[qs, :].astype(F32))).astype(o_ref.dtype)


def _moba_prompt(layer, depth, z3, tabs, n_ret, n_heads, kv_bufs):
    b, seq, _ = z3.shape
    base = 4 * n_ret
    col = lambda off: pl.BlockSpec((None, seq, HEAD_DIM), lambda bi, hi: (bi, 0, base + off + hi))
    tab = pl.BlockSpec((seq, HEAD_DIM), lambda bi, hi: (0, 0))
    n_pages = seq // PAGE_SIZE
    page_spec = pl.BlockSpec((None, None, n_pages, None, PAGE_SIZE, HEAD_DIM),
                             lambda bi, hi: (layer, bi, 0, hi, 0, 0))
    page_shape = jax.ShapeDtypeStruct((depth, b, n_pages, n_heads, PAGE_SIZE, HEAD_DIM), F32)
    in_specs = [col(0), col(n_heads), col(2 * n_heads), col(3 * n_heads), tab, tab, tab]
    aliases = {}
    if kv_bufs:
        aliases = {len(in_specs): 1, len(in_specs) + 1: 2}
        in_specs = in_specs + [pl.BlockSpec(memory_space=pl.ANY)] * 2
    return pl.pallas_call(
        functools.partial(_moba_prompt_body, seq=seq),
        out_shape=(jax.ShapeDtypeStruct((b, seq, n_heads * HEAD_DIM), MIX_DTYPE), page_shape, page_shape),
        grid=(b, n_heads),
        in_specs=in_specs,
        out_specs=(pl.BlockSpec((None, seq, HEAD_DIM), lambda bi, hi: (bi, 0, hi)), page_spec, page_spec),
        input_output_aliases=aliases,
        compiler_params=_cparams(("parallel", "parallel")),
        name="moba_prompt",
    )(z3, z3, z3, z3, *tabs, *kv_bufs)


def _decode_gate_body(mq_ref, mk_ref, mv_ref, bm_ref, c_ref, sa_ref, sb_ref, q_ref, k_ref, v_ref, idx_ref,
                      *, n_heads, n_full):
    c, sa, sb = c_ref[...], sa_ref[...], sb_ref[...]
    rows = mq_ref.shape[0]
    blk_i = lax.broadcasted_iota(jnp.int32, (n_full, LANES), 0)
    pick_row = lax.broadcasted_iota(jnp.int32, (SUBLANES, LANES), 0)
    for h in range(n_heads):
        cs = slice(h * HEAD_DIM, (h + 1) * HEAD_DIM)
        q = _moba_rot(mq_ref[:, cs].astype(F32), c, sa, sb)
        k = _moba_rot(mk_ref[:, cs].astype(F32), c, sa, sb)
        q_ref[h] = q
        k_ref[h] = k
        v_ref[h] = mv_ref[:, cs].astype(F32)
        q_pad = jnp.concatenate([q, jnp.zeros((LANES - rows, HEAD_DIM), F32)], axis=0)
        bm_h = bm_ref[pl.ds(h, n_full, stride=n_heads), :]
        gate = lax.dot_general(bm_h, q_pad, NT_DIMS, preferred_element_type=F32,
                               precision=lax.Precision.HIGHEST)
        picks = jnp.zeros((SUBLANES, LANES), jnp.int32)
        for r in range(MOBA_TOP_K):
            m = jnp.max(gate, axis=0, keepdims=True)
            pick = jnp.min(jnp.where(gate == m, blk_i, n_full), axis=0, keepdims=True)
            picks = jnp.where(pick_row == r, pick, picks)
            gate = jnp.where(blk_i == pick, -jnp.inf, gate)
        idx_ref[h] = picks


def _decode_gate(z3, bm, tabs, n_ret, n_heads):
    b, rows, _ = z3.shape
    n_full = bm.shape[1]
    bm = bm.reshape(b, n_full * n_heads, HEAD_DIM)
    w = n_heads * HEAD_DIM
    base = 4 * n_ret * HEAD_DIM // w
    col = lambda off: pl.BlockSpec((None, rows, w), lambda bi: (bi, 0, base + off))
    tab = pl.BlockSpec((rows, HEAD_DIM), lambda bi: (0, 0))
    qkv_shape = jax.ShapeDtypeStruct((b, n_heads, rows, HEAD_DIM), F32)
    qkv_spec = pl.BlockSpec((None, n_heads, rows, HEAD_DIM), lambda bi: (bi, 0, 0, 0))
    return pl.pallas_call(
        functools.partial(_decode_gate_body, n_heads=n_heads, n_full=n_full),
        out_shape=(qkv_shape, qkv_shape, qkv_shape,
                   jax.ShapeDtypeStruct((b, n_heads, SUBLANES, LANES), jnp.int32)),
        grid=(b,),
        in_specs=[col(0), col(1), col(2),
                  pl.BlockSpec((None, n_full * n_heads, HEAD_DIM), lambda bi: (bi, 0, 0)),
                  tab, tab, tab],
        out_specs=(qkv_spec, qkv_spec, qkv_spec,
                   pl.BlockSpec((None, n_heads, SUBLANES, LANES), lambda bi: (bi, 0, 0, 0))),
        compiler_params=_cparams(("parallel",)),
        name="decode_gate",
    )(z3, z3, z3, bm, *tabs)


def _decode_attn_body(idx_ref, pt_ref, *refs, n_tok):
    del idx_ref, pt_ref
    ppb = MOBA_BLOCK // PAGE_SIZE
    per_tok = MOBA_TOP_K * ppb
    n_slab = n_tok * per_tok
    k_slabs, v_slabs = refs[:n_slab], refs[n_slab:2 * n_slab]
    q_ref, k_ref, v_ref, mg_ref, o_ref = refs[2 * n_slab:]
    rows = q_ref.shape[0]
    tok_keys = per_tok * PAGE_SIZE
    qb = (q_ref[...] * (HEAD_DIM ** -0.5)).astype(BF16)
    ks = jnp.concatenate([r[...].astype(BF16) for r in k_slabs], axis=0)
    vs = jnp.concatenate([r[...].astype(BF16) for r in v_slabs], axis=0)
    s_sel = lax.dot_general(qb, ks, NT_DIMS, preferred_element_type=F32)
    row_lo = lax.broadcasted_iota(jnp.int32, s_sel.shape, 0) * tok_keys
    col_i = lax.broadcasted_iota(jnp.int32, s_sel.shape, 1)
    s_sel = jnp.where((col_i >= row_lo) & (col_i < row_lo + tok_keys), s_sel, -jnp.inf)
    pad = jnp.zeros((LANES - rows, HEAD_DIM), F32)
    k_own = jnp.concatenate([k_ref[...], pad], axis=0).astype(BF16)
    v_own = jnp.concatenate([v_ref[...], pad], axis=0).astype(BF16)
    s_own = lax.dot_general(qb, k_own, NT_DIMS, preferred_element_type=F32)
    tok_i = lax.broadcasted_iota(jnp.int32, (rows, LANES), 0)
    key_i = lax.broadcasted_iota(jnp.int32, (rows, LANES), 1)
    s_own = jnp.where((key_i <= tok_i) & (key_i < n_tok), s_own, -jnp.inf)
    m = jnp.maximum(jnp.max(s_sel, axis=-1, keepdims=True), jnp.max(s_own, axis=-1, keepdims=True))
    p_sel = jnp.exp(s_sel - m)
    p_own = jnp.exp(s_own - m)
    l = jnp.sum(p_sel, axis=-1, keepdims=True) + jnp.sum(p_own, axis=-1, keepdims=True)
    o = (jnp.dot(p_sel.astype(BF16), vs, preferred_element_type=F32)
         + jnp.dot(p_own.astype(BF16), v_own, preferred_element_type=F32)) / l
    o_ref[...] = (o * _silu(mg_ref[...].astype(F32))).astype(o_ref.dtype)


def _decode_attn(layer, cache_k, cache_v, idx_flat, page_table, q, k, v, z3, n_ret, n_tok):
    b, n_heads, rows, _ = q.shape
    ppb = MOBA_BLOCK // PAGE_SIZE

    def slab_spec(t, r, j):
        def imap(bi, hi, idx, pt):
            blk = idx[((bi * n_heads + hi) * MOBA_TOP_K + r) * n_tok + t]
            return (layer, pt[bi, blk * ppb + j], hi, 0, 0)
        return pl.BlockSpec((None, None, None, PAGE_SIZE, HEAD_DIM), imap)

    slabs = [slab_spec(t, r, j) for t in range(n_tok) for r in range(MOBA_TOP_K) for j in range(ppb)]
    tok_spec = pl.BlockSpec((None, None, rows, HEAD_DIM), lambda bi, hi, idx, pt: (bi, hi, 0, 0))
    gate_col = 4 * n_ret + 3 * n_heads
    return pl.pallas_call(
        functools.partial(_decode_attn_body, n_tok=n_tok),
        out_shape=jax.ShapeDtypeStruct((b, rows, n_heads * HEAD_DIM), MIX_DTYPE),
        grid_spec=pltpu.PrefetchScalarGridSpec(
            num_scalar_prefetch=2,
            grid=(b, n_heads),
            in_specs=slabs + slabs + [
                tok_spec, tok_spec, tok_spec,
                pl.BlockSpec((None, rows, HEAD_DIM), lambda bi, hi, idx, pt: (bi, 0, gate_col + hi))],
            out_specs=pl.BlockSpec((None, rows, HEAD_DIM), lambda bi, hi, idx, pt: (bi, 0, hi)),
        ),
        compiler_params=_cparams(("parallel", "parallel")),
        name="decode_attn",
    )(idx_flat, page_table, *([cache_k] * len(slabs)), *([cache_v] * len(slabs)), q, k, v, z3)


def _row_tile(m, cap):
    t = min(m, cap)
    while m % t:
        t //= 2
    return t


def kernel(x_prompt, x_sample, cache_k, cache_v, state_ret, page_table, p_prompt, p_sample,
           norm_mix, w_in, w_out, w_pe, w_pg, b_pg, norm_final):
    batch, seq, d_model = x_prompt.shape
    dec_batch, dec_seq, _ = x_sample.shape
    depth = w_in.shape[0]
    in_cols = w_in.shape[2]
    n_ret = in_cols // (8 * HEAD_DIM)
    n_moba = n_ret
    n_pages = page_table.shape[1]
    past = n_pages * PAGE_SIZE
    assert seq % MOBA_BLOCK == 0 and seq % RET_CHUNK == 0
    assert past % MOBA_BLOCK == 0 and past // MOBA_BLOCK >= MOBA_TOP_K
    assert dec_seq <= SUBLANES and RET_CHUNK % dec_seq == 0
    dec_rows = SUBLANES
    m_p = batch * seq
    m_s = dec_batch * dec_rows

    w_out_bf = w_out.astype(BF16)
    w_pg_bf = w_pg.astype(BF16)
    w_pe_bf = w_pe.astype(BF16)
    norms_in = norm_mix.reshape(depth, 1, d_model)
    norms_next = jnp.concatenate([norm_mix[1:], norm_final[None]], axis=0).reshape(depth, 1, d_model)
    bpg = b_pg.reshape(depth, 1, d_model)

    pos_p = jnp.arange(seq, dtype=jnp.int32)
    pos_s = past + jnp.arange(dec_rows, dtype=jnp.int32)
    ret_tab_p = _rope_tables(pos_p, HEAD_DIM, RET_THETA)
    ret_tab_s = _rope_tables(pos_s, HEAD_DIM, RET_THETA)
    ret_cs_p = (ret_tab_p[0], ret_tab_p[1] + ret_tab_p[2])
    ret_cs_s = (ret_tab_s[0], ret_tab_s[1] + ret_tab_s[2])
    moba_tab_p = _rope_tables(pos_p, ROT_DIM, ROPE_THETA)
    moba_tab_s = _rope_tables(pos_s, ROT_DIM, ROPE_THETA)
    dec_p = _ret_tables(n_ret, RET_CHUNK)
    dec_s = _ret_tables(n_ret, dec_seq)

    xp = x_prompt.reshape(m_p, d_model)
    xs = jnp.pad(x_sample, ((0, 0), (0, dec_rows - dec_seq), (0, 0))).reshape(m_s, d_model)
    pp = p_prompt.reshape(depth, m_p, -1)
    ps = jnp.pad(p_sample, ((0, 0), (0, 0), (0, dec_rows - dec_seq), (0, 0))).reshape(depth, m_s, -1)
    s0_prompt = jnp.zeros((1, batch, n_ret, HEAD_DIM, HEAD_DIM), F32)

    tm_in = _row_tile(m_p, 1024)
    tn = _row_tile(in_cols, 1024)
    tm_o = _row_tile(m_p, 256)
    pages_per_step = dec_batch * n_pages // ((in_cols // tn) * (m_p // tm_in))

    kv_bufs = ()
    hp = _prenorm(xp, norms_in, tm_o)
    hs = _prenorm(xs, norms_in, m_s)
    s_pr, k_sa, v_sa, s_sa = [], [], [], []
    for i in range(depth):
        last = i == depth - 1
        z, zs, block_means = _inproj(i, hp, hs, w_in, cache_k, page_table, tm_in, tn, pages_per_step)
        z = z.reshape(batch, seq, in_cols)
        zs = zs.reshape(dec_batch, dec_rows, in_cols)
        ret_o, s_new = _retention(z, *ret_cs_p, s0_prompt, 0, dec_p, n_ret, 1)
        att_o, *kv_bufs = _moba_prompt(i, depth, z, moba_tab_p, n_ret, n_moba, kv_bufs)
        outs = _outproj(i, xp, ret_o.reshape(m_p, -1), att_o.reshape(m_p, -1), pp,
                        w_out_bf, w_pg_bf, bpg, w_pe_bf, norms_next, tm_o, last)
        xp = outs[0]
        hp = None if last else outs[1]
        s_pr.append(s_new)
        ret_s, ss_new = _retention(zs, *ret_cs_s, state_ret, i, dec_s, n_ret, n_ret)
        q, k, v, idx = _decode_gate(zs, block_means, moba_tab_s, n_ret, n_moba)
        idx_flat = idx[:, :, :MOBA_TOP_K, :dec_seq].reshape(-1)
        att_s = _decode_attn(i, cache_k, cache_v, idx_flat, page_table, q, k, v, zs, n_ret, dec_seq)
        outs = _outproj(i, xs, ret_s.reshape(m_s, -1), att_s.reshape(m_s, -1), ps,
                        w_out_bf, w_pg_bf, bpg, w_pe_bf, norms_next, m_s, last)
        xs = outs[0]
        hs = None if last else outs[1]
        k_sa.append(k[:, :, :dec_seq])
        v_sa.append(v[:, :, :dec_seq])
        s_sa.append(ss_new)

    y_prompt = xp.reshape(batch, seq, d_model)
    y_sample = xs.reshape(dec_batch, dec_rows, d_model)[:, :dec_seq]
    return (y_prompt, y_sample, kv_bufs[0], kv_bufs[1], jnp.stack(s_pr),
            jnp.stack(k_sa), jnp.stack(v_sa), jnp.stack(s_sa))
```

```python
import functools
import math

import jax
import jax.numpy as jnp
from jax import lax
from jax.experimental import pallas as pl
from jax.experimental.pallas import tpu as pltpu

HEAD_DIM = 128
PAGE_SIZE = 128
RET_CHUNK = 128
RET_THETA = 10000.0
MOBA_BLOCK = 256
MOBA_TOP_K = 3
ROPE_THETA = 500000.0
ROT_DIM = HEAD_DIM // 4
NORM_EPS = 1e-6
LANES = 128
SUBLANES = 8
VMEM_LIMIT = 56 * 1024 * 1024

F32 = jnp.float32
BF16 = jnp.bfloat16
NT_DIMS = (((1,), (1,)), ((), ()))
LOG2E = math.log2(math.e)

Z_DTYPE = jnp.bfloat16
MIX_DTYPE = jnp.bfloat16


def _cparams(sem):
    return pltpu.CompilerParams(dimension_semantics=sem, vmem_limit_bytes=VMEM_LIMIT)


def _rms(x, w):
    return x * lax.rsqrt(jnp.mean(x * x, axis=-1, keepdims=True) + NORM_EPS) * w


def _silu(g):
    return g * jax.nn.sigmoid(g)


def _prenorm_body(x_ref, nw_ref, h_ref):
    h_ref[...] = _rms(x_ref[...], nw_ref[...]).astype(h_ref.dtype)


def _prenorm(x, norms, tm):
    m, d = x.shape
    return pl.pallas_call(
        _prenorm_body,
        out_shape=jax.ShapeDtypeStruct((m, d), BF16),
        grid=(m // tm,),
        in_specs=[pl.BlockSpec((tm, d), lambda i: (i, 0)),
                  pl.BlockSpec((None, 1, d), lambda i: (0, 0, 0))],
        out_specs=pl.BlockSpec((tm, d), lambda i: (i, 0)),
        compiler_params=_cparams(("parallel",)),
        name="prenorm",
    )(x, norms)


def _sum_pages(page_refs, o_ref):
    ppb = MOBA_BLOCK // PAGE_SIZE
    for n in range(len(page_refs) // ppb):
        acc = jnp.sum(page_refs[ppb * n][...], axis=1)
        for r in range(1, ppb):
            acc = acc + jnp.sum(page_refs[ppb * n + r][...], axis=1)
        o_ref[n] = acc * (1.0 / MOBA_BLOCK)


def _inproj_body(pt_ref, h_ref, hs_ref, w_ref, ck_ref, z_ref, zs_ref, bm_ref, wbf_ref, pbuf_ref, sem_ref,
                 *, layer, g, n_i, steps_per_b):
    step = pl.program_id(0) * n_i + pl.program_id(1)
    n_steps = pl.num_programs(0) * n_i

    def page_copy(st, r, slot):
        pg = pt_ref[lax.div(st, steps_per_b), lax.rem(st, steps_per_b) * g + r]
        return pltpu.make_async_copy(ck_ref.at[layer, pg], pbuf_ref.at[slot, r], sem_ref.at[slot])

    @pl.when(step == 0)
    def _():
        for r in range(g):
            page_copy(step, r, 0).start()

    @pl.when(step + 1 < n_steps)
    def _():
        for r in range(g):
            page_copy(step + 1, r, lax.rem(step + 1, 2)).start()

    @pl.when(pl.program_id(1) == 0)
    def _():
        wbf_ref[...] = w_ref[...].astype(BF16)
        zs_ref[...] = jnp.dot(hs_ref[...], wbf_ref[...], preferred_element_type=F32).astype(zs_ref.dtype)

    z_ref[...] = jnp.dot(h_ref[...], wbf_ref[...], preferred_element_type=F32).astype(z_ref.dtype)

    slot = lax.rem(step, 2)
    for r in range(g):
        page_copy(step, r, slot).wait()
    _sum_pages([pbuf_ref.at[slot, r] for r in range(g)], bm_ref)


def _inproj(layer, h, hs, w_in, cache_k, page_table, tm, tn, g):
    m, d = h.shape
    ms = hs.shape[0]
    n = w_in.shape[2]
    _, _, heads, _, hd = cache_k.shape
    b, n_pages = page_table.shape
    ppb = MOBA_BLOCK // PAGE_SIZE
    n_i = m // tm
    steps_per_b = n_pages // g
    assert (n // tn) * n_i * g == b * n_pages and n_pages % g == 0 and g % ppb == 0

    def bm_map(j, i, pt):
        s = j * n_i + i
        return (s // steps_per_b, s % steps_per_b, 0, 0)

    return pl.pallas_call(
        functools.partial(_inproj_body, layer=layer, g=g, n_i=n_i, steps_per_b=steps_per_b),
        out_shape=(jax.ShapeDtypeStruct((m, n), Z_DTYPE), jax.ShapeDtypeStruct((ms, n), Z_DTYPE),
                   jax.ShapeDtypeStruct((b, n_pages // ppb, heads, hd), F32)),
        grid_spec=pltpu.PrefetchScalarGridSpec(
            num_scalar_prefetch=1,
            grid=(n // tn, n_i),
            in_specs=[
                pl.BlockSpec((tm, d), lambda j, i, pt: (i, 0)),
                pl.BlockSpec((ms, d), lambda j, i, pt: (0, 0)),
                pl.BlockSpec((None, d, tn), lambda j, i, pt: (layer, 0, j)),
                pl.BlockSpec(memory_space=pl.ANY),
            ],
            out_specs=(pl.BlockSpec((tm, tn), lambda j, i, pt: (i, j)),
                       pl.BlockSpec((ms, tn), lambda j, i, pt: (0, j)),
                       pl.BlockSpec((None, g // ppb, heads, hd), bm_map)),
            scratch_shapes=[pltpu.VMEM((d, tn), BF16),
                            pltpu.VMEM((2, g, heads, PAGE_SIZE, hd), F32),
                            pltpu.SemaphoreType.DMA((2,))],
        ),
        compiler_params=_cparams(("arbitrary", "arbitrary")),
        name="inproj",
    )(page_table, h, hs, w_in, cache_k)


def _outproj_body(x_ref, ret_ref, att_ref, p_ref, wo_r_ref, wo_a_ref, wpg_ref, bpg_ref, wpe_ref,
                  nw_ref, *out_refs, last):
    x1 = (x_ref[...]
          + jnp.dot(ret_ref[...], wo_r_ref[...], preferred_element_type=F32)
          + jnp.dot(att_ref[...], wo_a_ref[...], preferred_element_type=F32))
    g = jax.nn.sigmoid(jnp.dot(x1.astype(BF16), wpg_ref[...], preferred_element_type=F32) + bpg_ref[...])
    pe = jnp.dot(p_ref[...].astype(BF16), wpe_ref[...], preferred_element_type=F32)
    x2 = x1 + g * pe
    y = _rms(x2, nw_ref[...])
    if last:
        out_refs[0][...] = y
    else:
        out_refs[0][...] = x2
        out_refs[1][...] = y.astype(out_refs[1].dtype)


def _outproj(layer, x, ret_m, att_m, p, wo_bf, wpg_bf, bpg, wpe_bf, norms_next, tm, last):
    m, d = x.shape
    rw = ret_m.shape[1]
    aw = att_m.shape[1]
    pd = p.shape[2]
    row = lambda w: pl.BlockSpec((tm, w), lambda i: (i, 0))
    lay = lambda r: pl.BlockSpec((None, r, d), lambda i: (layer, 0, 0))
    out_shape = [jax.ShapeDtypeStruct((m, d), F32)]
    out_specs = [row(d)]
    if not last:
        out_shape.append(jax.ShapeDtypeStruct((m, d), BF16))
        out_specs.append(row(d))
    return pl.pallas_call(
        functools.partial(_outproj_body, last=last),
        out_shape=tuple(out_shape),
        grid=(m // tm,),
        in_specs=[
            row(d), row(rw), row(aw),
            pl.BlockSpec((None, tm, pd), lambda i: (layer, i, 0)),
            lay(rw),
            pl.BlockSpec((None, aw, d), lambda i: (layer, rw // aw, 0)),
            lay(d), lay(1), lay(pd), lay(1),
        ],
        out_specs=tuple(out_specs),
        compiler_params=_cparams(("parallel",)),
        name="outproj",
    )(x, ret_m, att_m, p, wo_bf, wo_bf, wpg_bf, bpg, wpe_bf, norms_next)


def _ret_tables(n_heads, c_eff):
    c = RET_CHUNK
    log_g = jnp.log(1.0 - 2.0 ** (-5.0 - jnp.arange(n_heads, dtype=F32)))
    i = jnp.arange(c, dtype=F32)
    rel = i[:, None] - i[None, :]
    din = jnp.where(rel >= 0, jnp.exp(jnp.maximum(rel, 0.0)[None] * log_g[:, None, None]), 0.0)
    qd = jnp.exp((i + 1.0)[None, :] * log_g[:, None])
    kd = jnp.where(i[None, :] < c_eff, jnp.exp((c_eff - 1.0 - i)[None, :] * log_g[:, None]), 0.0)
    cd = jnp.exp(c_eff * log_g)
    bc = lambda t: jnp.broadcast_to(t[..., None], t.shape + (LANES,))
    return din, bc(qd), bc(kd), jnp.broadcast_to(cd[:, None, None], (n_heads, 1, LANES))


def _ret_body(rq_ref, rk_ref, rv_ref, rg_ref, cos_ref, sin_ref, s0_ref, din_ref, qd_ref, kd_ref, cd_ref,
              o_ref, s_ref, *, rows, heads):
    c = RET_CHUNK
    n_chunks = max(rows // c, 1)
    padded = n_chunks * c

    def full(ref, cs):
        t = ref[:, cs].astype(F32)
        if rows < padded:
            t = jnp.concatenate([t, jnp.zeros((padded - rows, HEAD_DIM), F32)], axis=0)
        return t

    one = slice(0, HEAD_DIM)
    cos = full(cos_ref, one)
    sin = full(sin_ref, one)
    chunks = lambda t: t.reshape(n_chunks, c, HEAD_DIM)
    for hh in range(heads):
        cs = slice(hh * HEAD_DIM, (hh + 1) * HEAD_DIM)
        din, qd, kd, cd = din_ref[hh], qd_ref[hh], kd_ref[hh], cd_ref[hh]
        q = full(rq_ref, cs)
        k = full(rk_ref, cs)
        q = q * cos + pltpu.roll(q, HEAD_DIM // 2, 1) * sin
        k = (k * cos + pltpu.roll(k, HEAD_DIM // 2, 1) * sin) * (HEAD_DIM ** -0.5)
        qb = chunks(q).astype(BF16)
        kb = chunks(k).astype(BF16)
        qdb = (chunks(q) * qd[None]).astype(BF16)
        kdec = chunks(k) * kd[None]
        vb = chunks(full(rv_ref, cs)).astype(BF16)
        s = s0_ref[hh]
        outs = []
        for n in range(n_chunks):
            scores = lax.dot_general(qb[n], kb[n], NT_DIMS, preferred_element_type=F32) * din
            outs.append(jnp.dot(scores.astype(BF16), vb[n], preferred_element_type=F32)
                        + jnp.dot(qdb[n], s.astype(BF16), preferred_element_type=F32))
            s = s * cd + jnp.dot(kdec[n].T.astype(BF16), vb[n], preferred_element_type=F32)
        s_ref[hh] = s
        o = outs[0] if n_chunks == 1 else jnp.concatenate(outs, axis=0)
        mu = jnp.mean(o, axis=-1, keepdims=True)
        var = jnp.mean(jnp.square(o - mu), axis=-1, keepdims=True)
        o = (o - mu) * lax.rsqrt(var + NORM_EPS)
        o = o * _silu(full(rg_ref, cs))
        o_ref[:, cs] = o[:rows].astype(o_ref.dtype)


def _retention(z3, cos, sin, s0, s0_layer, tables, n_heads, heads_per_step):
    b, rows, _ = z3.shape
    din, qd, kd, cd = tables
    c = RET_CHUNK
    hps = heads_per_step
    w = hps * HEAD_DIM
    groups = n_heads // hps
    col = lambda off: pl.BlockSpec((None, rows, w), lambda bi, gi: (bi, 0, off * groups + gi))
    tab = pl.BlockSpec((rows, HEAD_DIM), lambda bi, gi: (0, 0))
    per_head = lambda r: pl.BlockSpec((hps, r, LANES), lambda bi, gi: (gi, 0, 0))
    return pl.pallas_call(
        functools.partial(_ret_body, rows=rows, heads=hps),
        out_shape=(jax.ShapeDtypeStruct((b, rows, n_heads * HEAD_DIM), MIX_DTYPE),
                   jax.ShapeDtypeStruct((b, n_heads, HEAD_DIM, HEAD_DIM), F32)),
        grid=(b, groups),
        in_specs=[col(0), col(1), col(2), col(3), tab, tab,
                  pl.BlockSpec((None, None, hps, HEAD_DIM, HEAD_DIM), lambda bi, gi: (s0_layer, bi, gi, 0, 0)),
                  per_head(c), per_head(c), per_head(c), per_head(1)],
        out_specs=(pl.BlockSpec((None, rows, w), lambda bi, gi: (bi, 0, gi)),
                   pl.BlockSpec((None, hps, HEAD_DIM, HEAD_DIM), lambda bi, gi: (bi, gi, 0, 0))),
        compiler_params=_cparams(("parallel", "parallel")),
        name="retention",
    )(z3, z3, z3, z3, cos, sin, s0, din, qd, kd, cd)


def _rope_tables(pos, rot_dim, theta):
    half = rot_dim // 2
    freqs = theta ** (-jnp.arange(half, dtype=F32) / half)
    ang = pos.astype(F32)[:, None] * freqs[None, :]
    cos, sin = jnp.cos(ang), jnp.sin(ang)
    n = pos.shape[0]
    rest = HEAD_DIM - rot_dim
    zh = jnp.zeros((n, half), F32)
    c = jnp.concatenate([cos, cos, jnp.ones((n, rest), F32)], axis=1)
    sa = jnp.concatenate([-sin, zh, jnp.zeros((n, rest), F32)], axis=1)
    sb = jnp.concatenate([zh, sin, jnp.zeros((n, rest), F32)], axis=1)
    return c, sa, sb


def _moba_rot(x, c, sa, sb):
    half = ROT_DIM // 2
    return x * c + pltpu.roll(x, HEAD_DIM - half, 1) * sa + pltpu.roll(x, half, 1) * sb


def _moba_prompt_body(mq_ref, mk_ref, mv_ref, mg_ref, c_ref, sa_ref, sb_ref, *refs, seq):
    o_ref, nk_ref, nv_ref = refs[-3:]
    blk = MOBA_BLOCK
    nb = seq // blk
    c, sa, sb = c_ref[...], sa_ref[...], sb_ref[...]
    q = _moba_rot(mq_ref[...].astype(F32), c, sa, sb)
    k = _moba_rot(mk_ref[...].astype(F32), c, sa, sb)
    v = mv_ref[...].astype(F32)
    if len(nk_ref.shape) == 4:
        for ref, val in ((nk_ref, k), (nv_ref, v)):
            ref[0] = val.reshape(seq // PAGE_SIZE, PAGE_SIZE, HEAD_DIM)
            ref[1:] = jnp.zeros((ref.shape[0] - 1,) + ref.shape[1:], F32)
    else:
        nk_ref[...] = k.reshape(seq // PAGE_SIZE, PAGE_SIZE, HEAD_DIM)
        nv_ref[...] = v.reshape(seq // PAGE_SIZE, PAGE_SIZE, HEAD_DIM)

    k_mean = jnp.mean(k.reshape(nb, blk, HEAD_DIM), axis=1)
    gate = lax.dot_general(k_mean, q, NT_DIMS, preferred_element_type=F32,
                           precision=lax.Precision.HIGHEST)
    qb_all = (q * (HEAD_DIM ** -0.5 * LOG2E)).astype(BF16)
    kb_all = k.astype(BF16)
    vt_all = jnp.concatenate([v.T, jnp.ones((2 * SUBLANES, seq), F32)], axis=0).astype(BF16)
    blk_id = lax.broadcasted_iota(jnp.int32, (nb, blk), 0)
    key_i = lax.broadcasted_iota(jnp.int32, (blk, blk), 0)
    qry_i = lax.broadcasted_iota(jnp.int32, (blk, blk), 1)
    causal = key_i <= qry_i

    for qi in range(nb):
        qs = slice(qi * blk, (qi + 1) * blk)
        qb = qb_all[qs]
        if qi > MOBA_TOP_K:
            g = gate[:, qs]
            rank = jnp.zeros((nb, blk), jnp.int32)
            for j in range(qi):
                gj = g[j:j + 1, :]
                ahead = (gj > g) | ((gj == g) & (j < blk_id))
                rank = rank + ahead.astype(jnp.int32)
            bias = jnp.where(rank < MOBA_TOP_K, 0.0, -jnp.inf)
        tiles = []
        for j in range(qi + 1):
            st = lax.dot_general(kb_all[j * blk:(j + 1) * blk], qb, NT_DIMS,
                                 preferred_element_type=F32)
            if j == qi:
                st = jnp.where(causal, st, -jnp.inf)
            elif qi > MOBA_TOP_K:
                st = st + bias[j:j + 1, :]
            tiles.append(st)
        st = tiles[0] if qi == 0 else jnp.concatenate(tiles, axis=0)
        p = jnp.exp2(st - jnp.max(st, axis=0, keepdims=True)).astype(BF16)
        acc = jnp.dot(vt_all[:, :(qi + 1) * blk], p, preferred_element_type=F32)
        o = (acc[:HEAD_DIM] / acc[HEAD_DIM:HEAD_DIM + 1]).T
        o_ref[qs, :] = (o * _silu(mg_ref[qs, :].astype(F32))).astype(o_ref.dtype)


def _moba_prompt(layer, depth, z3, tabs, n_ret, n_heads, kv_bufs):
    b, seq, _ = z3.shape
    base = 4 * n_ret
    col = lambda off: pl.BlockSpec((None, seq, HEAD_DIM), lambda bi, hi: (bi, 0, base + off + hi))
    tab = pl.BlockSpec((seq, HEAD_DIM), lambda bi, hi: (0, 0))
    n_pages = seq // PAGE_SIZE
    page_spec = pl.BlockSpec((None, None, n_pages, None, PAGE_SIZE, HEAD_DIM),
                             lambda bi, hi: (layer, bi, 0, hi, 0, 0))
    page_shape = jax.ShapeDtypeStruct((depth, b, n_pages, n_heads, PAGE_SIZE, HEAD_DIM), F32)
    in_specs = [col(0), col(n_heads), col(2 * n_heads), col(3 * n_heads), tab, tab, tab]
    aliases = {}
    if kv_bufs:
        aliases = {len(in_specs): 1, len(in_specs) + 1: 2}
        in_specs = in_specs + [pl.BlockSpec(memory_space=pl.ANY)] * 2
    else:
        assert layer == 0
        page_spec = pl.BlockSpec((depth, None, n_pages, None, PAGE_SIZE, HEAD_DIM),
                                 lambda bi, hi: (0, bi, 0, hi, 0, 0))
    return pl.pallas_call(
        functools.partial(_moba_prompt_body, seq=seq),
        out_shape=(jax.ShapeDtypeStruct((b, seq, n_heads * HEAD_DIM), MIX_DTYPE), page_shape, page_shape),
        grid=(b, n_heads),
        in_specs=in_specs,
        out_specs=(pl.BlockSpec((None, seq, HEAD_DIM), lambda bi, hi: (bi, 0, hi)), page_spec, page_spec),
        input_output_aliases=aliases,
        compiler_params=_cparams(("parallel", "parallel")),
        name="moba_prompt",
    )(z3, z3, z3, z3, *tabs, *kv_bufs)


def _decode_gate_body(mq_ref, mk_ref, mv_ref, bm_ref, c_ref, sa_ref, sb_ref, q_ref, k_ref, v_ref, idx_ref,
                      *, n_heads, n_full):
    c, sa, sb = c_ref[...], sa_ref[...], sb_ref[...]
    rows = mq_ref.shape[0]
    blk_i = lax.broadcasted_iota(jnp.int32, (n_full, LANES), 0)
    pick_row = lax.broadcasted_iota(jnp.int32, (SUBLANES, LANES), 0)
    for h in range(n_heads):
        cs = slice(h * HEAD_DIM, (h + 1) * HEAD_DIM)
        q = _moba_rot(mq_ref[:, cs].astype(F32), c, sa, sb)
        k = _moba_rot(mk_ref[:, cs].astype(F32), c, sa, sb)
        q_ref[h] = q
        k_ref[h] = k
        v_ref[h] = mv_ref[:, cs].astype(F32)
        q_pad = jnp.concatenate([q, jnp.zeros((LANES - rows, HEAD_DIM), F32)], axis=0)
        bm_h = bm_ref[pl.ds(h, n_full, stride=n_heads), :]
        gate = lax.dot_general(bm_h, q_pad, NT_DIMS, preferred_element_type=F32,
                               precision=lax.Precision.HIGHEST)
        picks = jnp.zeros((SUBLANES, LANES), jnp.int32)
        for r in range(MOBA_TOP_K):
            m = jnp.max(gate, axis=0, keepdims=True)
            pick = jnp.min(jnp.where(gate == m, blk_i, n_full), axis=0, keepdims=True)
            picks = jnp.where(pick_row == r, pick, picks)
            gate = jnp.where(blk_i == pick, -jnp.inf, gate)
        idx_ref[h] = picks


def _decode_gate(z3, bm, tabs, n_ret, n_heads):
    b, rows, _ = z3.shape
    n_full = bm.shape[1]
    bm = bm.reshape(b, n_full * n_heads, HEAD_DIM)
    w = n_heads * HEAD_DIM
    base = 4 * n_ret * HEAD_DIM // w
    col = lambda off: pl.BlockSpec((None, rows, w), lambda bi: (bi, 0, base + off))
    tab = pl.BlockSpec((rows, HEAD_DIM), lambda bi: (0, 0))
    qkv_shape = jax.ShapeDtypeStruct((b, n_heads, rows, HEAD_DIM), F32)
    qkv_spec = pl.BlockSpec((None, n_heads, rows, HEAD_DIM), lambda bi: (bi, 0, 0, 0))
    return pl.pallas_call(
        functools.partial(_decode_gate_body, n_heads=n_heads, n_full=n_full),
        out_shape=(qkv_shape, qkv_shape, qkv_shape,
                   jax.ShapeDtypeStruct((b, n_heads, SUBLANES, LANES), jnp.int32)),
        grid=(b,),
        in_specs=[col(0), col(1), col(2),
                  pl.BlockSpec((None, n_full * n_heads, HEAD_DIM), lambda bi: (bi, 0, 0)),
                  tab, tab, tab],
        out_specs=(qkv_spec, qkv_spec, qkv_spec,
                   pl.BlockSpec((None, n_heads, SUBLANES, LANES), lambda bi: (bi, 0, 0, 0))),
        compiler_params=_cparams(("parallel",)),
        name="decode_gate",
    )(z3, z3, z3, bm, *tabs)


def _decode_attn_body(idx_ref, pt_ref, ck_ref, cv_ref, q_ref, k_ref, v_ref, mg_ref, o_ref,
                      kbuf_ref, vbuf_ref, ksem_ref, vsem_ref, *, layer, n_tok, heads, n_heads):
    ppb = MOBA_BLOCK // PAGE_SIZE
    per_tok = MOBA_TOP_K * ppb
    n_slab = n_tok * per_tok
    groups = n_heads // heads
    step = pl.program_id(0)
    n_steps = pl.num_programs(0)

    def start_gather(st, slot):
        bi = lax.div(st, groups)
        gi = lax.rem(st, groups)
        for hh in range(heads):
            hi = gi * heads + hh
            for t in range(n_tok):
                for r in range(MOBA_TOP_K):
                    blk = idx_ref[((bi * n_heads + hi) * MOBA_TOP_K + r) * n_tok + t]
                    for j in range(ppb):
                        pg = pt_ref[bi, blk * ppb + j]
                        dst = pl.ds((hh * n_slab + (t * MOBA_TOP_K + r) * ppb + j) * PAGE_SIZE, PAGE_SIZE)
                        pltpu.make_async_copy(ck_ref.at[layer, pg, hi], kbuf_ref.at[slot, dst],
                                              ksem_ref.at[slot]).start()
                        pltpu.make_async_copy(cv_ref.at[layer, pg, hi], vbuf_ref.at[slot, dst],
                                              vsem_ref.at[slot]).start()

    @pl.when(step == 0)
    def _():
        start_gather(step, 0)

    @pl.when(step + 1 < n_steps)
    def _():
        start_gather(step + 1, lax.rem(step + 1, 2))

    slot = lax.rem(step, 2)
    for i in range(heads * n_slab):
        dst = pl.ds(i * PAGE_SIZE, PAGE_SIZE)
        pltpu.make_async_copy(ck_ref.at[layer, 0, 0], kbuf_ref.at[slot, dst], ksem_ref.at[slot]).wait()
        pltpu.make_async_copy(cv_ref.at[layer, 0, 0], vbuf_ref.at[slot, dst], vsem_ref.at[slot]).wait()

    rows = q_ref.shape[1]
    tok_keys = per_tok * PAGE_SIZE
    pad = jnp.zeros((LANES - rows, HEAD_DIM), F32)
    tok_i = lax.broadcasted_iota(jnp.int32, (rows, LANES), 0)
    key_i = lax.broadcasted_iota(jnp.int32, (rows, LANES), 1)
    own_ok = (key_i <= tok_i) & (key_i < n_tok)
    row_lo = lax.broadcasted_iota(jnp.int32, (rows, n_tok * tok_keys), 0) * tok_keys
    col_i = lax.broadcasted_iota(jnp.int32, (rows, n_tok * tok_keys), 1)
    sel_ok = (col_i >= row_lo) & (col_i < row_lo + tok_keys)
    for hh in range(heads):
        span = pl.ds(hh * n_slab * PAGE_SIZE, n_slab * PAGE_SIZE)
        qb = (q_ref[hh] * (HEAD_DIM ** -0.5)).astype(BF16)
        ks = kbuf_ref[slot, span, :].astype(BF16)
        vs = vbuf_ref[slot, span, :].astype(BF16)
        s_sel = lax.dot_general(qb, ks, NT_DIMS, preferred_element_type=F32)
        s_sel = jnp.where(sel_ok, s_sel, -jnp.inf)
        k_own = jnp.concatenate([k_ref[hh], pad], axis=0).astype(BF16)
        v_own = jnp.concatenate([v_ref[hh], pad], axis=0).astype(BF16)
        s_own = lax.dot_general(qb, k_own, NT_DIMS, preferred_element_type=F32)
        s_own = jnp.where(own_ok, s_own, -jnp.inf)
        m = jnp.maximum(jnp.max(s_sel, axis=-1, keepdims=True), jnp.max(s_own, axis=-1, keepdims=True))
        p_sel = jnp.exp(s_sel - m)
        p_own = jnp.exp(s_own - m)
        l = jnp.sum(p_sel, axis=-1, keepdims=True) + jnp.sum(p_own, axis=-1, keepdims=True)
        o = (jnp.dot(p_sel.astype(BF16), vs, preferred_element_type=F32)
             + jnp.dot(p_own.astype(BF16), v_own, preferred_element_type=F32)) / l
        cs = slice(hh * HEAD_DIM, (hh + 1) * HEAD_DIM)
        o_ref[:, cs] = (o * _silu(mg_ref[:, cs].astype(F32))).astype(o_ref.dtype)


DECODE_HEADS_PER_STEP = 2


def _decode_attn(layer, cache_k, cache_v, idx_flat, page_table, q, k, v, z3, n_ret, n_tok):
    b, n_heads, rows, _ = q.shape
    ppb = MOBA_BLOCK // PAGE_SIZE
    hps = math.gcd(DECODE_HEADS_PER_STEP, n_heads)
    groups = n_heads // hps
    w = hps * HEAD_DIM
    buf_rows = hps * n_tok * MOBA_TOP_K * ppb * PAGE_SIZE
    tok_spec = pl.BlockSpec((None, hps, rows, HEAD_DIM), lambda s, idx, pt: (s // groups, s % groups, 0, 0))
    gate_col = (4 * n_ret + 3 * n_heads) // hps
    any_spec = pl.BlockSpec(memory_space=pl.ANY)
    return pl.pallas_call(
        functools.partial(_decode_attn_body, layer=layer, n_tok=n_tok, heads=hps, n_heads=n_heads),
        out_shape=jax.ShapeDtypeStruct((b, rows, n_heads * HEAD_DIM), MIX_DTYPE),
        grid_spec=pltpu.PrefetchScalarGridSpec(
            num_scalar_prefetch=2,
            grid=(b * groups,),
            in_specs=[any_spec, any_spec, tok_spec, tok_spec, tok_spec,
                      pl.BlockSpec((None, rows, w), lambda s, idx, pt: (s // groups, 0, gate_col + s % groups))],
            out_specs=pl.BlockSpec((None, rows, w), lambda s, idx, pt: (s // groups, 0, s % groups)),
            scratch_shapes=[pltpu.VMEM((2, buf_rows, HEAD_DIM), F32), pltpu.VMEM((2, buf_rows, HEAD_DIM), F32),
                            pltpu.SemaphoreType.DMA((2,)), pltpu.SemaphoreType.DMA((2,))],
        ),
        compiler_params=_cparams(("arbitrary",)),
        name="decode_attn",
    )(idx_flat, page_table, cache_k, cache_v, q, k, v, z3)


def _row_tile(m, cap):
    t = min(m, cap)
    while m % t:
        t //= 2
    return t


def kernel(x_prompt, x_sample, cache_k, cache_v, state_ret, page_table, p_prompt, p_sample,
           norm_mix, w_in, w_out, w_pe, w_pg, b_pg, norm_final):
    batch, seq, d_model = x_prompt.shape
    dec_batch, dec_seq, _ = x_sample.shape
    depth = w_in.shape[0]
    in_cols = w_in.shape[2]
    n_ret = in_cols // (8 * HEAD_DIM)
    n_moba = n_ret
    n_pages = page_table.shape[1]
    past = n_pages * PAGE_SIZE
    assert seq % MOBA_BLOCK == 0 and seq % RET_CHUNK == 0
    assert past % MOBA_BLOCK == 0 and past // MOBA_BLOCK >= MOBA_TOP_K
    assert dec_seq <= SUBLANES and RET_CHUNK % dec_seq == 0
    dec_rows = SUBLANES
    m_p = batch * seq
    m_s = dec_batch * dec_rows

    w_out_bf = w_out.astype(BF16)
    w_pg_bf = w_pg.astype(BF16)
    w_pe_bf = w_pe.astype(BF16)
    norms_in = norm_mix.reshape(depth, 1, d_model)
    norms_next = jnp.concatenate([norm_mix[1:], norm_final[None]], axis=0).reshape(depth, 1, d_model)
    bpg = b_pg.reshape(depth, 1, d_model)

    pos_p = jnp.arange(seq, dtype=jnp.int32)
    pos_s = past + jnp.arange(dec_rows, dtype=jnp.int32)
    ret_tab_p = _rope_tables(pos_p, HEAD_DIM, RET_THETA)
    ret_tab_s = _rope_tables(pos_s, HEAD_DIM, RET_THETA)
    ret_cs_p = (ret_tab_p[0], ret_tab_p[1] + ret_tab_p[2])
    ret_cs_s = (ret_tab_s[0], ret_tab_s[1] + ret_tab_s[2])
    moba_tab_p = _rope_tables(pos_p, ROT_DIM, ROPE_THETA)
    moba_tab_s = _rope_tables(pos_s, ROT_DIM, ROPE_THETA)
    dec_p = _ret_tables(n_ret, RET_CHUNK)
    dec_s = _ret_tables(n_ret, dec_seq)

    xp = x_prompt.reshape(m_p, d_model)
    xs = jnp.pad(x_sample, ((0, 0), (0, dec_rows - dec_seq), (0, 0))).reshape(m_s, d_model)
    pp = p_prompt.reshape(depth, m_p, -1)
    ps = jnp.pad(p_sample, ((0, 0), (0, 0), (0, dec_rows - dec_seq), (0, 0))).reshape(depth, m_s, -1)
    s0_prompt = jnp.zeros((1, batch, n_ret, HEAD_DIM, HEAD_DIM), F32)

    tm_in = _row_tile(m_p, 1024)
    tn = _row_tile(in_cols, 1024)
    tm_o = _row_tile(m_p, 256)
    pages_per_step = dec_batch * n_pages // ((in_cols // tn) * (m_p // tm_in))

    kv_bufs = ()
    hp = _prenorm(xp, norms_in, tm_o)
    hs = _prenorm(xs, norms_in, m_s)
    s_pr, k_sa, v_sa, s_sa = [], [], [], []
    for i in range(depth):
        last = i == depth - 1
        z, zs, block_means = _inproj(i, hp, hs, w_in, cache_k, page_table, tm_in, tn, pages_per_step)
        z = z.reshape(batch, seq, in_cols)
        zs = zs.reshape(dec_batch, dec_rows, in_cols)
        ret_o, s_new = _retention(z, *ret_cs_p, s0_prompt, 0, dec_p, n_ret, 1)
        att_o, *kv_bufs = _moba_prompt(i, depth, z, moba_tab_p, n_ret, n_moba, kv_bufs)
        outs = _outproj(i, xp, ret_o.reshape(m_p, -1), att_o.reshape(m_p, -1), pp,
                        w_out_bf, w_pg_bf, bpg, w_pe_bf, norms_next, tm_o, last)
        xp = outs[0]
        hp = None if last else outs[1]
        s_pr.append(s_new)
        ret_s, ss_new = _retention(zs, *ret_cs_s, state_ret, i, dec_s, n_ret, n_ret)
        q, k, v, idx = _decode_gate(zs, block_means, moba_tab_s, n_ret, n_moba)
        idx_flat = idx[:, :, :MOBA_TOP_K, :dec_seq].reshape(-1)
        att_s = _decode_attn(i, cache_k, cache_v, idx_flat, page_table, q, k, v, zs, n_ret, dec_seq)
        outs = _outproj(i, xs, ret_s.reshape(m_s, -1), att_s.reshape(m_s, -1), ps,
                        w_out_bf, w_pg_bf, bpg, w_pe_bf, norms_next, m_s, last)
        xs = outs[0]
        hs = None if last else outs[1]
        k_sa.append(k[:, :, :dec_seq])
        v_sa.append(v[:, :, :dec_seq])
        s_sa.append(ss_new)

    y_prompt = xp.reshape(batch, seq, d_model)
    y_sample = xs.reshape(dec_batch, dec_rows, d_model)[:, :dec_seq]
    return (y_prompt, y_sample, kv_bufs[0], kv_bufs[1], jnp.stack(s_pr),
            jnp.stack(k_sa), jnp.stack(v_sa), jnp.stack(s_sa))
```

```python
import functools
import math

import jax
import jax.numpy as jnp
from jax import lax
from jax.experimental import pallas as pl
from jax.experimental.pallas import tpu as pltpu

HEAD_DIM = 128
PAGE_SIZE = 128
RET_CHUNK = 128
RET_THETA = 10000.0
MOBA_BLOCK = 256
MOBA_TOP_K = 3
ROPE_THETA = 500000.0
ROT_DIM = HEAD_DIM // 4
NORM_EPS = 1e-6
LANES = 128
SUBLANES = 8
VMEM_LIMIT = 56 * 1024 * 1024

F32 = jnp.float32
BF16 = jnp.bfloat16
NT_DIMS = (((1,), (1,)), ((), ()))
LOG2E = math.log2(math.e)

Z_DTYPE = jnp.bfloat16
MIX_DTYPE = jnp.bfloat16


def _cparams(sem):
    return pltpu.CompilerParams(dimension_semantics=sem, vmem_limit_bytes=VMEM_LIMIT)


def _rms(x, w):
    return x * lax.rsqrt(jnp.mean(x * x, axis=-1, keepdims=True) + NORM_EPS) * w


def _silu(g):
    return g * jax.nn.sigmoid(g)


def _prenorm_body(x_ref, nw_ref, h_ref):
    h_ref[...] = _rms(x_ref[...], nw_ref[...]).astype(h_ref.dtype)


def _prenorm(x, norms, tm):
    m, d = x.shape
    return pl.pallas_call(
        _prenorm_body,
        out_shape=jax.ShapeDtypeStruct((m, d), BF16),
        grid=(m // tm,),
        in_specs=[pl.BlockSpec((tm, d), lambda i: (i, 0)),
                  pl.BlockSpec((None, 1, d), lambda i: (0, 0, 0))],
        out_specs=pl.BlockSpec((tm, d), lambda i: (i, 0)),
        compiler_params=_cparams(("parallel",)),
        name="prenorm",
    )(x, norms)


def _sum_pages(page_refs, o_ref):
    ppb = MOBA_BLOCK // PAGE_SIZE
    for n in range(len(page_refs) // ppb):
        acc = jnp.sum(page_refs[ppb * n][...], axis=1)
        for r in range(1, ppb):
            acc = acc + jnp.sum(page_refs[ppb * n + r][...], axis=1)
        o_ref[n] = acc * (1.0 / MOBA_BLOCK)


def _inproj_body(pt_ref, h_ref, hs_ref, w_ref, ck_ref, z_ref, zs_ref, bm_ref, wbf_ref, pbuf_ref, sem_ref,
                 *, layer, g, n_i, steps_per_b):
    step = pl.program_id(0) * n_i + pl.program_id(1)
    n_steps = pl.num_programs(0) * n_i

    def page_copy(st, r, slot):
        pg = pt_ref[lax.div(st, steps_per_b), lax.rem(st, steps_per_b) * g + r]
        return pltpu.make_async_copy(ck_ref.at[layer, pg], pbuf_ref.at[slot, r], sem_ref.at[slot])

    @pl.when(step == 0)
    def _():
        for r in range(g):
            page_copy(step, r, 0).start()

    @pl.when(step + 1 < n_steps)
    def _():
        for r in range(g):
            page_copy(step + 1, r, lax.rem(step + 1, 2)).start()

    @pl.when(pl.program_id(1) == 0)
    def _():
        wbf_ref[...] = w_ref[...].astype(BF16)
        zs_ref[...] = jnp.dot(hs_ref[...], wbf_ref[...], preferred_element_type=F32).astype(zs_ref.dtype)

    slot = lax.rem(step, 2)
    for r in range(g):
        page_copy(step, r, slot).wait()
    _sum_pages([pbuf_ref.at[slot, r] for r in range(g)], bm_ref)
    z_ref[...] = jnp.dot(h_ref[...], wbf_ref[...], preferred_element_type=F32).astype(z_ref.dtype)


def _inproj(layer, h, hs, w_in, cache_k, page_table, tm, tn, g):
    m, d = h.shape
    ms = hs.shape[0]
    n = w_in.shape[2]
    _, _, heads, _, hd = cache_k.shape
    b, n_pages = page_table.shape
    ppb = MOBA_BLOCK // PAGE_SIZE
    n_i = m // tm
    steps_per_b = n_pages // g
    assert (n // tn) * n_i * g == b * n_pages and n_pages % g == 0 and g % ppb == 0

    def bm_map(j, i, pt):
        s = j * n_i + i
        return (s // steps_per_b, s % steps_per_b, 0, 0)

    return pl.pallas_call(
        functools.partial(_inproj_body, layer=layer, g=g, n_i=n_i, steps_per_b=steps_per_b),
        out_shape=(jax.ShapeDtypeStruct((m, n), Z_DTYPE), jax.ShapeDtypeStruct((ms, n), Z_DTYPE),
                   jax.ShapeDtypeStruct((b, n_pages // ppb, heads, hd), F32)),
        grid_spec=pltpu.PrefetchScalarGridSpec(
            num_scalar_prefetch=1,
            grid=(n // tn, n_i),
            in_specs=[
                pl.BlockSpec((tm, d), lambda j, i, pt: (i, 0)),
                pl.BlockSpec((ms, d), lambda j, i, pt: (0, 0)),
                pl.BlockSpec((None, d, tn), lambda j, i, pt: (layer, 0, j)),
                pl.BlockSpec(memory_space=pl.ANY),
            ],
            out_specs=(pl.BlockSpec((tm, tn), lambda j, i, pt: (i, j)),
                       pl.BlockSpec((ms, tn), lambda j, i, pt: (0, j)),
                       pl.BlockSpec((None, g // ppb, heads, hd), bm_map)),
            scratch_shapes=[pltpu.VMEM((d, tn), BF16),
                            pltpu.VMEM((2, g, heads, PAGE_SIZE, hd), F32),
                            pltpu.SemaphoreType.DMA((2,))],
        ),
        compiler_params=_cparams(("arbitrary", "arbitrary")),
        name="inproj",
    )(page_table, h, hs, w_in, cache_k)


def _outproj_body(x_ref, ret_ref, att_ref, p_ref, xs_ref, rets_ref, atts_ref, ps_ref,
                  wo_r_ref, wo_a_ref, wpg_ref, bpg_ref, wpe_ref, nw_ref, *out_refs, last):
    n_out = len(out_refs) // 2

    def mix(x_r, ret_r, att_r, p_r, outs):
        x1 = (x_r[...]
              + jnp.dot(ret_r[...], wo_r_ref[...], preferred_element_type=F32)
              + jnp.dot(att_r[...], wo_a_ref[...], preferred_element_type=F32))
        g = jax.nn.sigmoid(jnp.dot(x1.astype(BF16), wpg_ref[...], preferred_element_type=F32) + bpg_ref[...])
        pe = jnp.dot(p_r[...].astype(BF16), wpe_ref[...], preferred_element_type=F32)
        x2 = x1 + g * pe
        y = _rms(x2, nw_ref[...])
        if last:
            outs[0][...] = y
        else:
            outs[0][...] = x2
            outs[1][...] = y.astype(outs[1].dtype)

    @pl.when(pl.program_id(0) == 0)
    def _():
        mix(xs_ref, rets_ref, atts_ref, ps_ref, out_refs[n_out:])

    mix(x_ref, ret_ref, att_ref, p_ref, out_refs[:n_out])


def _outproj(layer, x, ret_m, att_m, p, xs, ret_s, att_s, ps, wo_bf, wpg_bf, bpg, wpe_bf, norms_next, tm, last):
    m, d = x.shape
    ms = xs.shape[0]
    rw = ret_m.shape[1]
    aw = att_m.shape[1]
    pd = p.shape[2]
    row = lambda w: pl.BlockSpec((tm, w), lambda i: (i, 0))
    whole = lambda w: pl.BlockSpec((ms, w), lambda i: (0, 0))
    lay = lambda r: pl.BlockSpec((None, r, d), lambda i: (layer, 0, 0))
    dtypes = [F32] if last else [F32, BF16]
    out_shape = [jax.ShapeDtypeStruct((m, d), t) for t in dtypes] + [jax.ShapeDtypeStruct((ms, d), t) for t in dtypes]
    out_specs = [row(d) for _ in dtypes] + [whole(d) for _ in dtypes]
    return pl.pallas_call(
        functools.partial(_outproj_body, last=last),
        out_shape=tuple(out_shape),
        grid=(m // tm,),
        in_specs=[
            row(d), row(rw), row(aw),
            pl.BlockSpec((None, tm, pd), lambda i: (layer, i, 0)),
            whole(d), whole(rw), whole(aw),
            pl.BlockSpec((None, ms, pd), lambda i: (layer, 0, 0)),
            lay(rw),
            pl.BlockSpec((None, aw, d), lambda i: (layer, rw // aw, 0)),
            lay(d), lay(1), lay(pd), lay(1),
        ],
        out_specs=tuple(out_specs),
        compiler_params=_cparams(("arbitrary",)),
        name="outproj",
    )(x, ret_m, att_m, p, xs, ret_s, att_s, ps, wo_bf, wo_bf, wpg_bf, bpg, wpe_bf, norms_next)


def _ret_tables(n_heads, c_eff):
    c = RET_CHUNK
    log_g = jnp.log(1.0 - 2.0 ** (-5.0 - jnp.arange(n_heads, dtype=F32)))
    i = jnp.arange(c, dtype=F32)
    rel = i[:, None] - i[None, :]
    din = jnp.where(rel >= 0, jnp.exp(jnp.maximum(rel, 0.0)[None] * log_g[:, None, None]), 0.0)
    qd = jnp.exp((i + 1.0)[None, :] * log_g[:, None])
    kd = jnp.where(i[None, :] < c_eff, jnp.exp((c_eff - 1.0 - i)[None, :] * log_g[:, None]), 0.0)
    cd = jnp.exp(c_eff * log_g)
    bc = lambda t: jnp.broadcast_to(t[..., None], t.shape + (LANES,))
    return din, bc(qd), bc(kd), jnp.broadcast_to(cd[:, None, None], (n_heads, 1, LANES))


def _ret_body(rq_ref, rk_ref, rv_ref, rg_ref, cos_ref, sin_ref, s0_ref, din_ref, qd_ref, kd_ref, cd_ref,
              o_ref, s_ref, *, rows, heads):
    c = RET_CHUNK
    n_chunks = max(rows // c, 1)
    padded = n_chunks * c

    def full(ref, cs):
        t = ref[:, cs].astype(F32)
        if rows < padded:
            t = jnp.concatenate([t, jnp.zeros((padded - rows, HEAD_DIM), F32)], axis=0)
        return t

    one = slice(0, HEAD_DIM)
    cos = full(cos_ref, one)
    sin = full(sin_ref, one)
    chunks = lambda t: t.reshape(n_chunks, c, HEAD_DIM)
    for hh in range(heads):
        cs = slice(hh * HEAD_DIM, (hh + 1) * HEAD_DIM)
        din, qd, kd, cd = din_ref[hh], qd_ref[hh], kd_ref[hh], cd_ref[hh]
        q = full(rq_ref, cs)
        k = full(rk_ref, cs)
        q = q * cos + pltpu.roll(q, HEAD_DIM // 2, 1) * sin
        k = (k * cos + pltpu.roll(k, HEAD_DIM // 2, 1) * sin) * (HEAD_DIM ** -0.5)
        qb = chunks(q).astype(BF16)
        kb = chunks(k).astype(BF16)
        qdb = (chunks(q) * qd[None]).astype(BF16)
        kdec = chunks(k) * kd[None]
        vb = chunks(full(rv_ref, cs)).astype(BF16)
        s = s0_ref[hh]
        outs = []
        for n in range(n_chunks):
            scores = lax.dot_general(qb[n], kb[n], NT_DIMS, preferred_element_type=F32) * din
            outs.append(jnp.dot(scores.astype(BF16), vb[n], preferred_element_type=F32)
                        + jnp.dot(qdb[n], s.astype(BF16), preferred_element_type=F32))
            s = s * cd + jnp.dot(kdec[n].T.astype(BF16), vb[n], preferred_element_type=F32)
        s_ref[hh] = s
        o = outs[0] if n_chunks == 1 else jnp.concatenate(outs, axis=0)
        mu = jnp.mean(o, axis=-1, keepdims=True)
        var = jnp.mean(jnp.square(o - mu), axis=-1, keepdims=True)
        o = (o - mu) * lax.rsqrt(var + NORM_EPS)
        o = o * _silu(full(rg_ref, cs))
        o_ref[:, cs] = o[:rows].astype(o_ref.dtype)


def _retention(z3, cos, sin, s0, s0_layer, tables, n_heads, heads_per_step):
    b, rows, _ = z3.shape
    din, qd, kd, cd = tables
    c = RET_CHUNK
    hps = heads_per_step
    w = hps * HEAD_DIM
    groups = n_heads // hps
    col = lambda off: pl.BlockSpec((None, rows, w), lambda bi, gi: (bi, 0, off * groups + gi))
    tab = pl.BlockSpec((rows, HEAD_DIM), lambda bi, gi: (0, 0))
    per_head = lambda r: pl.BlockSpec((hps, r, LANES), lambda bi, gi: (gi, 0, 0))
    return pl.pallas_call(
        functools.partial(_ret_body, rows=rows, heads=hps),
        out_shape=(jax.ShapeDtypeStruct((b, rows, n_heads * HEAD_DIM), MIX_DTYPE),
                   jax.ShapeDtypeStruct((b, n_heads, HEAD_DIM, HEAD_DIM), F32)),
        grid=(b, groups),
        in_specs=[col(0), col(1), col(2), col(3), tab, tab,
                  pl.BlockSpec((None, None, hps, HEAD_DIM, HEAD_DIM), lambda bi, gi: (s0_layer, bi, gi, 0, 0)),
                  per_head(c), per_head(c), per_head(c), per_head(1)],
        out_specs=(pl.BlockSpec((None, rows, w), lambda bi, gi: (bi, 0, gi)),
                   pl.BlockSpec((None, hps, HEAD_DIM, HEAD_DIM), lambda bi, gi: (bi, gi, 0, 0))),
        compiler_params=_cparams(("parallel", "parallel")),
        name="retention",
    )(z3, z3, z3, z3, cos, sin, s0, din, qd, kd, cd)


def _rope_tables(pos, rot_dim, theta):
    half = rot_dim // 2
    freqs = theta ** (-jnp.arange(half, dtype=F32) / half)
    ang = pos.astype(F32)[:, None] * freqs[None, :]
    cos, sin = jnp.cos(ang), jnp.sin(ang)
    n = pos.shape[0]
    rest = HEAD_DIM - rot_dim
    zh = jnp.zeros((n, half), F32)
    c = jnp.concatenate([cos, cos, jnp.ones((n, rest), F32)], axis=1)
    sa = jnp.concatenate([-sin, zh, jnp.zeros((n, rest), F32)], axis=1)
    sb = jnp.concatenate([zh, sin, jnp.zeros((n, rest), F32)], axis=1)
    return c, sa, sb


def _moba_rot(x, c, sa, sb):
    half = ROT_DIM // 2
    return x * c + pltpu.roll(x, HEAD_DIM - half, 1) * sa + pltpu.roll(x, half, 1) * sb


def _moba_prompt_body(mq_ref, mk_ref, mv_ref, mg_ref, c_ref, sa_ref, sb_ref, *refs, seq):
    o_ref, nk_ref, nv_ref = refs[-3:]
    blk = MOBA_BLOCK
    nb = seq // blk
    c, sa, sb = c_ref[...], sa_ref[...], sb_ref[...]
    q = _moba_rot(mq_ref[...].astype(F32), c, sa, sb)
    k = _moba_rot(mk_ref[...].astype(F32), c, sa, sb)
    v = mv_ref[...].astype(F32)
    if len(nk_ref.shape) == 4:
        for ref, val in ((nk_ref, k), (nv_ref, v)):
            ref[0] = val.reshape(seq // PAGE_SIZE, PAGE_SIZE, HEAD_DIM)
            ref[1:] = jnp.zeros((ref.shape[0] - 1,) + ref.shape[1:], F32)
    else:
        nk_ref[...] = k.reshape(seq // PAGE_SIZE, PAGE_SIZE, HEAD_DIM)
        nv_ref[...] = v.reshape(seq // PAGE_SIZE, PAGE_SIZE, HEAD_DIM)

    k_mean = jnp.mean(k.reshape(nb, blk, HEAD_DIM), axis=1)
    gate = lax.dot_general(k_mean, q, NT_DIMS, preferred_element_type=F32,
                           precision=lax.Precision.HIGHEST)
    qb_all = (q * (HEAD_DIM ** -0.5 * LOG2E)).astype(BF16)
    kb_all = k.astype(BF16)
    vt_all = jnp.concatenate([v.T, jnp.ones((2 * SUBLANES, seq), F32)], axis=0).astype(BF16)
    blk_id = lax.broadcasted_iota(jnp.int32, (nb, blk), 0)
    key_i = lax.broadcasted_iota(jnp.int32, (blk, blk), 0)
    qry_i = lax.broadcasted_iota(jnp.int32, (blk, blk), 1)
    causal = key_i <= qry_i

    for qi in range(nb):
        qs = slice(qi * blk, (qi + 1) * blk)
        qb = qb_all[qs]
        if qi > MOBA_TOP_K:
            g = gate[:, qs]
            rank = jnp.zeros((nb, blk), jnp.int32)
            for j in range(qi):
                gj = g[j:j + 1, :]
                ahead = (gj > g) | ((gj == g) & (j < blk_id))
                rank = rank + ahead.astype(jnp.int32)
            bias = jnp.where(rank < MOBA_TOP_K, 0.0, -jnp.inf)
        tiles = []
        for j in range(qi + 1):
            st = lax.dot_general(kb_all[j * blk:(j + 1) * blk], qb, NT_DIMS,
                                 preferred_element_type=F32)
            if j == qi:
                st = jnp.where(causal, st, -jnp.inf)
            elif qi > MOBA_TOP_K:
                st = st + bias[j:j + 1, :]
            tiles.append(st)
        st = tiles[0] if qi == 0 else jnp.concatenate(tiles, axis=0)
        p = jnp.exp2(st - jnp.max(st, axis=0, keepdims=True)).astype(BF16)
        acc = jnp.dot(vt_all[:, :(qi + 1) * blk], p, preferred_element_type=F32)
        o = (acc[:HEAD_DIM] / acc[HEAD_DIM:HEAD_DIM + 1]).T
        o_ref[qs, :] = (o * _silu(mg_ref[qs, :].astype(F32))).astype(o_ref.dtype)


def _moba_prompt(layer, depth, z3, tabs, n_ret, n_heads, kv_bufs):
    b, seq, _ = z3.shape
    base = 4 * n_ret
    col = lambda off: pl.BlockSpec((None, seq, HEAD_DIM), lambda bi, hi: (bi, 0, base + off + hi))
    tab = pl.BlockSpec((seq, HEAD_DIM), lambda bi, hi: (0, 0))
    n_pages = seq // PAGE_SIZE
    page_spec = pl.BlockSpec((None, None, n_pages, None, PAGE_SIZE, HEAD_DIM),
                             lambda bi, hi: (layer, bi, 0, hi, 0, 0))
    page_shape = jax.ShapeDtypeStruct((depth, b, n_pages, n_heads, PAGE_SIZE, HEAD_DIM), F32)
    in_specs = [col(0), col(n_heads), col(2 * n_heads), col(3 * n_heads), tab, tab, tab]
    aliases = {}
    if kv_bufs:
        aliases = {len(in_specs): 1, len(in_specs) + 1: 2}
        in_specs = in_specs + [pl.BlockSpec(memory_space=pl.ANY)] * 2
    else:
        assert layer == 0
        page_spec = pl.BlockSpec((depth, None, n_pages, None, PAGE_SIZE, HEAD_DIM),
                                 lambda bi, hi: (0, bi, 0, hi, 0, 0))
    return pl.pallas_call(
        functools.partial(_moba_prompt_body, seq=seq),
        out_shape=(jax.ShapeDtypeStruct((b, seq, n_heads * HEAD_DIM), MIX_DTYPE), page_shape, page_shape),
        grid=(b, n_heads),
        in_specs=in_specs,
        out_specs=(pl.BlockSpec((None, seq, HEAD_DIM), lambda bi, hi: (bi, 0, hi)), page_spec, page_spec),
        input_output_aliases=aliases,
        compiler_params=_cparams(("parallel", "parallel")),
        name="moba_prompt",
    )(z3, z3, z3, z3, *tabs, *kv_bufs)


def _decode_gate_body(mq_ref, mk_ref, mv_ref, bm_ref, c_ref, sa_ref, sb_ref, q_ref, k_ref, v_ref, idx_ref,
                      *, n_heads, n_full):
    c, sa, sb = c_ref[...], sa_ref[...], sb_ref[...]
    rows = mq_ref.shape[0]
    blk_i = lax.broadcasted_iota(jnp.int32, (n_full, LANES), 0)
    pick_row = lax.broadcasted_iota(jnp.int32, (SUBLANES, LANES), 0)
    for h in range(n_heads):
        cs = slice(h * HEAD_DIM, (h + 1) * HEAD_DIM)
        q = _moba_rot(mq_ref[:, cs].astype(F32), c, sa, sb)
        k = _moba_rot(mk_ref[:, cs].astype(F32), c, sa, sb)
        q_ref[h] = q
        k_ref[h] = k
        v_ref[h] = mv_ref[:, cs].astype(F32)
        q_pad = jnp.concatenate([q, jnp.zeros((LANES - rows, HEAD_DIM), F32)], axis=0)
        bm_h = bm_ref[pl.ds(h, n_full, stride=n_heads), :]
        gate = lax.dot_general(bm_h, q_pad, NT_DIMS, preferred_element_type=F32,
                               precision=lax.Precision.HIGHEST)
        picks = jnp.zeros((SUBLANES, LANES), jnp.int32)
        for r in range(MOBA_TOP_K):
            m = jnp.max(gate, axis=0, keepdims=True)
            pick = jnp.min(jnp.where(gate == m, blk_i, n_full), axis=0, keepdims=True)
            picks = jnp.where(pick_row == r, pick, picks)
            gate = jnp.where(blk_i == pick, -jnp.inf, gate)
        idx_ref[h] = picks


def _decode_gate(z3, bm, tabs, n_ret, n_heads):
    b, rows, _ = z3.shape
    n_full = bm.shape[1]
    bm = bm.reshape(b, n_full * n_heads, HEAD_DIM)
    w = n_heads * HEAD_DIM
    base = 4 * n_ret * HEAD_DIM // w
    col = lambda off: pl.BlockSpec((None, rows, w), lambda bi: (bi, 0, base + off))
    tab = pl.BlockSpec((rows, HEAD_DIM), lambda bi: (0, 0))
    qkv_shape = jax.ShapeDtypeStruct((b, n_heads, rows, HEAD_DIM), F32)
    qkv_spec = pl.BlockSpec((None, n_heads, rows, HEAD_DIM), lambda bi: (bi, 0, 0, 0))
    return pl.pallas_call(
        functools.partial(_decode_gate_body, n_heads=n_heads, n_full=n_full),
        out_shape=(qkv_shape, qkv_shape, qkv_shape,
                   jax.ShapeDtypeStruct((b, n_heads, SUBLANES, LANES), jnp.int32)),
        grid=(b,),
        in_specs=[col(0), col(1), col(2),
                  pl.BlockSpec((None, n_full * n_heads, HEAD_DIM), lambda bi: (bi, 0, 0)),
                  tab, tab, tab],
        out_specs=(qkv_spec, qkv_spec, qkv_spec,
                   pl.BlockSpec((None, n_heads, SUBLANES, LANES), lambda bi: (bi, 0, 0, 0))),
        compiler_params=_cparams(("parallel",)),
        name="decode_gate",
    )(z3, z3, z3, bm, *tabs)


def _decode_attn_body(idx_ref, pt_ref, ck_ref, cv_ref, q_ref, k_ref, v_ref, mg_ref, o_ref,
                      kbuf_ref, vbuf_ref, ksem_ref, vsem_ref, *, layer, n_tok, heads, n_heads):
    ppb = MOBA_BLOCK // PAGE_SIZE
    per_tok = MOBA_TOP_K * ppb
    n_slab = n_tok * per_tok
    groups = n_heads // heads
    step = pl.program_id(0)
    n_steps = pl.num_programs(0)

    def start_gather(st, slot):
        bi = lax.div(st, groups)
        gi = lax.rem(st, groups)
        for hh in range(heads):
            hi = gi * heads + hh
            for t in range(n_tok):
                for r in range(MOBA_TOP_K):
                    blk = idx_ref[((bi * n_heads + hi) * MOBA_TOP_K + r) * n_tok + t]
                    for j in range(ppb):
                        pg = pt_ref[bi, blk * ppb + j]
                        dst = pl.ds((hh * n_slab + (t * MOBA_TOP_K + r) * ppb + j) * PAGE_SIZE, PAGE_SIZE)
                        pltpu.make_async_copy(ck_ref.at[layer, pg, hi], kbuf_ref.at[slot, dst],
                                              ksem_ref.at[slot]).start()
                        pltpu.make_async_copy(cv_ref.at[layer, pg, hi], vbuf_ref.at[slot, dst],
                                              vsem_ref.at[slot]).start()

    @pl.when(step == 0)
    def _():
        start_gather(step, 0)

    @pl.when(step + 1 < n_steps)
    def _():
        start_gather(step + 1, lax.rem(step + 1, 2))

    slot = lax.rem(step, 2)
    for i in range(heads * n_slab):
        dst = pl.ds(i * PAGE_SIZE, PAGE_SIZE)
        pltpu.make_async_copy(ck_ref.at[layer, 0, 0], kbuf_ref.at[slot, dst], ksem_ref.at[slot]).wait()
        pltpu.make_async_copy(cv_ref.at[layer, 0, 0], vbuf_ref.at[slot, dst], vsem_ref.at[slot]).wait()

    rows = q_ref.shape[1]
    tok_keys = per_tok * PAGE_SIZE
    pad = jnp.zeros((LANES - rows, HEAD_DIM), F32)
    tok_i = lax.broadcasted_iota(jnp.int32, (rows, LANES), 0)
    key_i = lax.broadcasted_iota(jnp.int32, (rows, LANES), 1)
    own_ok = (key_i <= tok_i) & (key_i < n_tok)
    row_lo = lax.broadcasted_iota(jnp.int32, (rows, n_tok * tok_keys), 0) * tok_keys
    col_i = lax.broadcasted_iota(jnp.int32, (rows, n_tok * tok_keys), 1)
    sel_ok = (col_i >= row_lo) & (col_i < row_lo + tok_keys)
    for hh in range(heads):
        span = pl.ds(hh * n_slab * PAGE_SIZE, n_slab * PAGE_SIZE)
        qb = (q_ref[hh] * (HEAD_DIM ** -0.5)).astype(BF16)
        ks = kbuf_ref[slot, span, :].astype(BF16)
        vs = vbuf_ref[slot, span, :].astype(BF16)
        s_sel = lax.dot_general(qb, ks, NT_DIMS, preferred_element_type=F32)
        s_sel = jnp.where(sel_ok, s_sel, -jnp.inf)
        k_own = jnp.concatenate([k_ref[hh], pad], axis=0).astype(BF16)
        v_own = jnp.concatenate([v_ref[hh], pad], axis=0).astype(BF16)
        s_own = lax.dot_general(qb, k_own, NT_DIMS, preferred_element_type=F32)
        s_own = jnp.where(own_ok, s_own, -jnp.inf)
        m = jnp.maximum(jnp.max(s_sel, axis=-1, keepdims=True), jnp.max(s_own, axis=-1, keepdims=True))
        p_sel = jnp.exp(s_sel - m)
        p_own = jnp.exp(s_own - m)
        l = jnp.sum(p_sel, axis=-1, keepdims=True) + jnp.sum(p_own, axis=-1, keepdims=True)
        o = (jnp.dot(p_sel.astype(BF16), vs, preferred_element_type=F32)
             + jnp.dot(p_own.astype(BF16), v_own, preferred_element_type=F32)) / l
        cs = slice(hh * HEAD_DIM, (hh + 1) * HEAD_DIM)
        o_ref[:, cs] = (o * _silu(mg_ref[:, cs].astype(F32))).astype(o_ref.dtype)


DECODE_HEADS_PER_STEP = 4


def _decode_attn(layer, cache_k, cache_v, idx_flat, page_table, q, k, v, z3, n_ret, n_tok):
    b, n_heads, rows, _ = q.shape
    ppb = MOBA_BLOCK // PAGE_SIZE
    hps = math.gcd(DECODE_HEADS_PER_STEP, n_heads)
    groups = n_heads // hps
    w = hps * HEAD_DIM
    buf_rows = hps * n_tok * MOBA_TOP_K * ppb * PAGE_SIZE
    tok_spec = pl.BlockSpec((None, hps, rows, HEAD_DIM), lambda s, idx, pt: (s // groups, s % groups, 0, 0))
    gate_col = (4 * n_ret + 3 * n_heads) // hps
    any_spec = pl.BlockSpec(memory_space=pl.ANY)
    return pl.pallas_call(
        functools.partial(_decode_attn_body, layer=layer, n_tok=n_tok, heads=hps, n_heads=n_heads),
        out_shape=jax.ShapeDtypeStruct((b, rows, n_heads * HEAD_DIM), MIX_DTYPE),
        grid_spec=pltpu.PrefetchScalarGridSpec(
            num_scalar_prefetch=2,
            grid=(b * groups,),
            in_specs=[any_spec, any_spec, tok_spec, tok_spec, tok_spec,
                      pl.BlockSpec((None, rows, w), lambda s, idx, pt: (s // groups, 0, gate_col + s % groups))],
            out_specs=pl.BlockSpec((None, rows, w), lambda s, idx, pt: (s // groups, 0, s % groups)),
            scratch_shapes=[pltpu.VMEM((2, buf_rows, HEAD_DIM), F32), pltpu.VMEM((2, buf_rows, HEAD_DIM), F32),
                            pltpu.SemaphoreType.DMA((2,)), pltpu.SemaphoreType.DMA((2,))],
        ),
        compiler_params=_cparams(("arbitrary",)),
        name="decode_attn",
    )(idx_flat, page_table, cache_k, cache_v, q, k, v, z3)


def _row_tile(m, cap):
    t = min(m, cap)
    while m % t:
        t //= 2
    return t


def kernel(x_prompt, x_sample, cache_k, cache_v, state_ret, page_table, p_prompt, p_sample,
           norm_mix, w_in, w_out, w_pe, w_pg, b_pg, norm_final):
    batch, seq, d_model = x_prompt.shape
    dec_batch, dec_seq, _ = x_sample.shape
    depth = w_in.shape[0]
    in_cols = w_in.shape[2]
    n_ret = in_cols // (8 * HEAD_DIM)
    n_moba = n_ret
    n_pages = page_table.shape[1]
    past = n_pages * PAGE_SIZE
    assert seq % MOBA_BLOCK == 0 and seq % RET_CHUNK == 0
    assert past % MOBA_BLOCK == 0 and past // MOBA_BLOCK >= MOBA_TOP_K
    assert dec_seq <= SUBLANES and RET_CHUNK % dec_seq == 0
    dec_rows = SUBLANES
    m_p = batch * seq
    m_s = dec_batch * dec_rows

    w_out_bf = w_out.astype(BF16)
    w_pg_bf = w_pg.astype(BF16)
    w_pe_bf = w_pe.astype(BF16)
    norms_in = norm_mix.reshape(depth, 1, d_model)
    norms_next = jnp.concatenate([norm_mix[1:], norm_final[None]], axis=0).reshape(depth, 1, d_model)
    bpg = b_pg.reshape(depth, 1, d_model)

    pos_p = jnp.arange(seq, dtype=jnp.int32)
    pos_s = past + jnp.arange(dec_rows, dtype=jnp.int32)
    ret_tab_p = _rope_tables(pos_p, HEAD_DIM, RET_THETA)
    ret_tab_s = _rope_tables(pos_s, HEAD_DIM, RET_THETA)
    ret_cs_p = (ret_tab_p[0], ret_tab_p[1] + ret_tab_p[2])
    ret_cs_s = (ret_tab_s[0], ret_tab_s[1] + ret_tab_s[2])
    moba_tab_p = _rope_tables(pos_p, ROT_DIM, ROPE_THETA)
    moba_tab_s = _rope_tables(pos_s, ROT_DIM, ROPE_THETA)
    dec_p = _ret_tables(n_ret, RET_CHUNK)
    dec_s = _ret_tables(n_ret, dec_seq)

    xp = x_prompt.reshape(m_p, d_model)
    xs = jnp.pad(x_sample, ((0, 0), (0, dec_rows - dec_seq), (0, 0))).reshape(m_s, d_model)
    pp = p_prompt.reshape(depth, m_p, -1)
    ps = jnp.pad(p_sample, ((0, 0), (0, 0), (0, dec_rows - dec_seq), (0, 0))).reshape(depth, m_s, -1)
    s0_prompt = jnp.zeros((1, batch, n_ret, HEAD_DIM, HEAD_DIM), F32)

    tm_in = _row_tile(m_p, 1024)
    tn = _row_tile(in_cols, 1024)
    tm_o = _row_tile(m_p, 256)
    pages_per_step = dec_batch * n_pages // ((in_cols // tn) * (m_p // tm_in))

    kv_bufs = ()
    hp = _prenorm(xp, norms_in, tm_o)
    hs = _prenorm(xs, norms_in, m_s)
    s_pr, k_sa, v_sa, s_sa = [], [], [], []
    for i in range(depth):
        last = i == depth - 1
        z, zs, block_means = _inproj(i, hp, hs, w_in, cache_k, page_table, tm_in, tn, pages_per_step)
        z = z.reshape(batch, seq, in_cols)
        zs = zs.reshape(dec_batch, dec_rows, in_cols)
        ret_o, s_new = _retention(z, *ret_cs_p, s0_prompt, 0, dec_p, n_ret, 1)
        att_o, *kv_bufs = _moba_prompt(i, depth, z, moba_tab_p, n_ret, n_moba, kv_bufs)
        s_pr.append(s_new)
        ret_s, ss_new = _retention(zs, *ret_cs_s, state_ret, i, dec_s, n_ret, n_ret)
        q, k, v, idx = _decode_gate(zs, block_means, moba_tab_s, n_ret, n_moba)
        idx_flat = idx[:, :, :MOBA_TOP_K, :dec_seq].reshape(-1)
        att_s = _decode_attn(i, cache_k, cache_v, idx_flat, page_table, q, k, v, zs, n_ret, dec_seq)
        outs = _outproj(i, xp, ret_o.reshape(m_p, -1), att_o.reshape(m_p, -1), pp,
                        xs, ret_s.reshape(m_s, -1), att_s.reshape(m_s, -1), ps,
                        w_out_bf, w_pg_bf, bpg, w_pe_bf, norms_next, tm_o, last)
        if last:
            xp, xs = outs
        else:
            xp, hp, xs, hs = outs
        k_sa.append(k[:, :, :dec_seq])
        v_sa.append(v[:, :, :dec_seq])
        s_sa.append(ss_new)

    y_prompt = xp.reshape(batch, seq, d_model)
    y_sample = xs.reshape(dec_batch, dec_rows, d_model)[:, :dec_seq]
    return (y_prompt, y_sample, kv_bufs[0], kv_bufs[1], jnp.stack(s_pr),
            jnp.stack(k_sa), jnp.stack(v_sa), jnp.stack(s_sa))
```

```python
import functools
import math

import jax
import jax.numpy as jnp
from jax import lax
from jax.experimental import pallas as pl
from jax.experimental.pallas import tpu as pltpu

HEAD_DIM = 128
PAGE_SIZE = 128
RET_CHUNK = 128
RET_THETA = 10000.0
MOBA_BLOCK = 256
MOBA_TOP_K = 3
ROPE_THETA = 500000.0
ROT_DIM = HEAD_DIM // 4
NORM_EPS = 1e-6
LANES = 128
SUBLANES = 8
VMEM_LIMIT = 56 * 1024 * 1024

F32 = jnp.float32
BF16 = jnp.bfloat16
NT_DIMS = (((1,), (1,)), ((), ()))
LOG2E = math.log2(math.e)

Z_DTYPE = jnp.bfloat16
MIX_DTYPE = jnp.bfloat16


def _cparams(sem):
    return pltpu.CompilerParams(dimension_semantics=sem, vmem_limit_bytes=VMEM_LIMIT)


def _rms(x, w):
    return x * lax.rsqrt(jnp.mean(x * x, axis=-1, keepdims=True) + NORM_EPS) * w


def _silu(g):
    return g * jax.nn.sigmoid(g)


def _prenorm_body(x_ref, nw_ref, h_ref):
    h_ref[...] = _rms(x_ref[...], nw_ref[...]).astype(h_ref.dtype)


def _prenorm(x, norms, tm):
    m, d = x.shape
    return pl.pallas_call(
        _prenorm_body,
        out_shape=jax.ShapeDtypeStruct((m, d), BF16),
        grid=(m // tm,),
        in_specs=[pl.BlockSpec((tm, d), lambda i: (i, 0)),
                  pl.BlockSpec((None, 1, d), lambda i: (0, 0, 0))],
        out_specs=pl.BlockSpec((tm, d), lambda i: (i, 0)),
        compiler_params=_cparams(("parallel",)),
        name="prenorm",
    )(x, norms)


def _sum_pages(page_refs, o_ref):
    ppb = MOBA_BLOCK // PAGE_SIZE
    for n in range(len(page_refs) // ppb):
        acc = jnp.sum(page_refs[ppb * n][...], axis=1)
        for r in range(1, ppb):
            acc = acc + jnp.sum(page_refs[ppb * n + r][...], axis=1)
        o_ref[n] = acc * (1.0 / MOBA_BLOCK)


def _inproj_body(pt_ref, h_ref, hs_ref, w_ref, ck_ref, z_ref, zs_ref, bm_ref, wbf_ref, pbuf_ref, sem_ref,
                 *, layer, g, n_i, steps_per_b):
    step = pl.program_id(0) * n_i + pl.program_id(1)
    n_steps = pl.num_programs(0) * n_i

    def page_copy(st, r, slot):
        pg = pt_ref[lax.div(st, steps_per_b), lax.rem(st, steps_per_b) * g + r]
        return pltpu.make_async_copy(ck_ref.at[layer, pg], pbuf_ref.at[slot, r], sem_ref.at[slot])

    @pl.when(step == 0)
    def _():
        for r in range(g):
            page_copy(step, r, 0).start()

    @pl.when(step + 1 < n_steps)
    def _():
        for r in range(g):
            page_copy(step + 1, r, lax.rem(step + 1, 2)).start()

    @pl.when(pl.program_id(1) == 0)
    def _():
        wbf_ref[...] = w_ref[...].astype(BF16)
        zs_ref[...] = jnp.dot(hs_ref[...], wbf_ref[...], preferred_element_type=F32).astype(zs_ref.dtype)

    z_ref[...] = jnp.dot(h_ref[...], wbf_ref[...], preferred_element_type=F32).astype(z_ref.dtype)

    slot = lax.rem(step, 2)
    for r in range(g):
        page_copy(step, r, slot).wait()
    _sum_pages([pbuf_ref.at[slot, r] for r in range(g)], bm_ref)


def _inproj(layer, h, hs, w_in, cache_k, page_table, tm, tn, g):
    m, d = h.shape
    ms = hs.shape[0]
    n = w_in.shape[2]
    _, _, heads, _, hd = cache_k.shape
    b, n_pages = page_table.shape
    ppb = MOBA_BLOCK // PAGE_SIZE
    n_i = m // tm
    steps_per_b = n_pages // g
    assert (n // tn) * n_i * g == b * n_pages and n_pages % g == 0 and g % ppb == 0

    def bm_map(j, i, pt):
        s = j * n_i + i
        return (s // steps_per_b, s % steps_per_b, 0, 0)

    return pl.pallas_call(
        functools.partial(_inproj_body, layer=layer, g=g, n_i=n_i, steps_per_b=steps_per_b),
        out_shape=(jax.ShapeDtypeStruct((m, n), Z_DTYPE), jax.ShapeDtypeStruct((ms, n), Z_DTYPE),
                   jax.ShapeDtypeStruct((b, n_pages // ppb, heads, hd), F32)),
        grid_spec=pltpu.PrefetchScalarGridSpec(
            num_scalar_prefetch=1,
            grid=(n // tn, n_i),
            in_specs=[
                pl.BlockSpec((tm, d), lambda j, i, pt: (i, 0)),
                pl.BlockSpec((ms, d), lambda j, i, pt: (0, 0)),
                pl.BlockSpec((None, d, tn), lambda j, i, pt: (layer, 0, j)),
                pl.BlockSpec(memory_space=pl.ANY),
            ],
            out_specs=(pl.BlockSpec((tm, tn), lambda j, i, pt: (i, j)),
                       pl.BlockSpec((ms, tn), lambda j, i, pt: (0, j)),
                       pl.BlockSpec((None, g // ppb, heads, hd), bm_map)),
            scratch_shapes=[pltpu.VMEM((d, tn), BF16),
                            pltpu.VMEM((2, g, heads, PAGE_SIZE, hd), F32),
                            pltpu.SemaphoreType.DMA((2,))],
        ),
        compiler_params=_cparams(("arbitrary", "arbitrary")),
        name="inproj",
    )(page_table, h, hs, w_in, cache_k)


def _outproj_body(x_ref, ret_ref, att_ref, p_ref, xs_ref, rets_ref, atts_ref, ps_ref,
                  wo_r_ref, wo_a_ref, wpg_ref, bpg_ref, wpe_ref, nw_ref, *out_refs, last):
    n_out = len(out_refs) // 2

    def mix(x_r, ret_r, att_r, p_r, outs):
        x1 = (x_r[...]
              + jnp.dot(ret_r[...], wo_r_ref[...], preferred_element_type=F32)
              + jnp.dot(att_r[...], wo_a_ref[...], preferred_element_type=F32))
        g = jax.nn.sigmoid(jnp.dot(x1.astype(BF16), wpg_ref[...], preferred_element_type=F32) + bpg_ref[...])
        pe = jnp.dot(p_r[...].astype(BF16), wpe_ref[...], preferred_element_type=F32)
        x2 = x1 + g * pe
        y = _rms(x2, nw_ref[...])
        if last:
            outs[0][...] = y
        else:
            outs[0][...] = x2
            outs[1][...] = y.astype(outs[1].dtype)

    @pl.when(pl.program_id(0) == 0)
    def _():
        mix(xs_ref, rets_ref, atts_ref, ps_ref, out_refs[n_out:])

    mix(x_ref, ret_ref, att_ref, p_ref, out_refs[:n_out])


def _outproj(layer, x, ret_m, att_m, p, xs, ret_s, att_s, ps, wo_bf, wpg_bf, bpg, wpe_bf, norms_next, tm, last):
    m, d = x.shape
    ms = xs.shape[0]
    rw = ret_m.shape[1]
    aw = att_m.shape[1]
    pd = p.shape[2]
    row = lambda w: pl.BlockSpec((tm, w), lambda i: (i, 0))
    whole = lambda w: pl.BlockSpec((ms, w), lambda i: (0, 0))
    lay = lambda r: pl.BlockSpec((None, r, d), lambda i: (layer, 0, 0))
    dtypes = [F32] if last else [F32, BF16]
    out_shape = [jax.ShapeDtypeStruct((m, d), t) for t in dtypes] + [jax.ShapeDtypeStruct((ms, d), t) for t in dtypes]
    out_specs = [row(d) for _ in dtypes] + [whole(d) for _ in dtypes]
    return pl.pallas_call(
        functools.partial(_outproj_body, last=last),
        out_shape=tuple(out_shape),
        grid=(m // tm,),
        in_specs=[
            row(d), row(rw), row(aw),
            pl.BlockSpec((None, tm, pd), lambda i: (layer, i, 0)),
            whole(d), whole(rw), whole(aw),
            pl.BlockSpec((None, ms, pd), lambda i: (layer, 0, 0)),
            lay(rw),
            pl.BlockSpec((None, aw, d), lambda i: (layer, rw // aw, 0)),
            lay(d), lay(1), lay(pd), lay(1),
        ],
        out_specs=tuple(out_specs),
        compiler_params=_cparams(("arbitrary",)),
        name="outproj",
    )(x, ret_m, att_m, p, xs, ret_s, att_s, ps, wo_bf, wo_bf, wpg_bf, bpg, wpe_bf, norms_next)


def _ret_tables(n_heads, c_eff):
    c = RET_CHUNK
    log_g = jnp.log(1.0 - 2.0 ** (-5.0 - jnp.arange(n_heads, dtype=F32)))
    i = jnp.arange(c, dtype=F32)
    rel = i[:, None] - i[None, :]
    din = jnp.where(rel >= 0, jnp.exp(jnp.maximum(rel, 0.0)[None] * log_g[:, None, None]), 0.0)
    qd = jnp.exp((i + 1.0)[None, :] * log_g[:, None])
    kd = jnp.where(i[None, :] < c_eff, jnp.exp((c_eff - 1.0 - i)[None, :] * log_g[:, None]), 0.0)
    cd = jnp.exp(c_eff * log_g)
    bc = lambda t: jnp.broadcast_to(t[..., None], t.shape + (LANES,))
    return din, bc(qd), bc(kd), jnp.broadcast_to(cd[:, None, None], (n_heads, 1, LANES))


def _ret_body(rq_ref, rk_ref, rv_ref, rg_ref, cos_ref, sin_ref, s0_ref, din_ref, qd_ref, kd_ref, cd_ref,
              o_ref, s_ref, *, rows, heads):
    c = RET_CHUNK
    n_chunks = max(rows // c, 1)
    padded = n_chunks * c

    def full(ref, cs):
        t = ref[:, cs].astype(F32)
        if rows < padded:
            t = jnp.concatenate([t, jnp.zeros((padded - rows, HEAD_DIM), F32)], axis=0)
        return t

    one = slice(0, HEAD_DIM)
    cos = full(cos_ref, one)
    sin = full(sin_ref, one)
    chunks = lambda t: t.reshape(n_chunks, c, HEAD_DIM)
    for hh in range(heads):
        cs = slice(hh * HEAD_DIM, (hh + 1) * HEAD_DIM)
        din, qd, kd, cd = din_ref[hh], qd_ref[hh], kd_ref[hh], cd_ref[hh]
        q = full(rq_ref, cs)
        k = full(rk_ref, cs)
        q = q * cos + pltpu.roll(q, HEAD_DIM // 2, 1) * sin
        k = (k * cos + pltpu.roll(k, HEAD_DIM // 2, 1) * sin) * (HEAD_DIM ** -0.5)
        qb = chunks(q).astype(BF16)
        kb = chunks(k).astype(BF16)
        qdb = (chunks(q) * qd[None]).astype(BF16)
        kdec = chunks(k) * kd[None]
        vb = chunks(full(rv_ref, cs)).astype(BF16)
        s = s0_ref[hh]
        outs = []
        for n in range(n_chunks):
            scores = lax.dot_general(qb[n], kb[n], NT_DIMS, preferred_element_type=F32) * din
            outs.append(jnp.dot(scores.astype(BF16), vb[n], preferred_element_type=F32)
                        + jnp.dot(qdb[n], s.astype(BF16), preferred_element_type=F32))
            s = s * cd + jnp.dot(kdec[n].T.astype(BF16), vb[n], preferred_element_type=F32)
        s_ref[hh] = s
        o = outs[0] if n_chunks == 1 else jnp.concatenate(outs, axis=0)
        mu = jnp.mean(o, axis=-1, keepdims=True)
        var = jnp.mean(jnp.square(o - mu), axis=-1, keepdims=True)
        o = (o - mu) * lax.rsqrt(var + NORM_EPS)
        o = o * _silu(full(rg_ref, cs))
        o_ref[:, cs] = o[:rows].astype(o_ref.dtype)


def _retention(z3, cos, sin, s0, s0_layer, tables, n_heads, heads_per_step):
    b, rows, _ = z3.shape
    din, qd, kd, cd = tables
    c = RET_CHUNK
    hps = heads_per_step
    w = hps * HEAD_DIM
    groups = n_heads // hps
    col = lambda off: pl.BlockSpec((None, rows, w), lambda bi, gi: (bi, 0, off * groups + gi))
    tab = pl.BlockSpec((rows, HEAD_DIM), lambda bi, gi: (0, 0))
    per_head = lambda r: pl.BlockSpec((hps, r, LANES), lambda bi, gi: (gi, 0, 0))
    return pl.pallas_call(
        functools.partial(_ret_body, rows=rows, heads=hps),
        out_shape=(jax.ShapeDtypeStruct((b, rows, n_heads * HEAD_DIM), MIX_DTYPE),
                   jax.ShapeDtypeStruct((b, n_heads, HEAD_DIM, HEAD_DIM), F32)),
        grid=(b, groups),
        in_specs=[col(0), col(1), col(2), col(3), tab, tab,
                  pl.BlockSpec((None, None, hps, HEAD_DIM, HEAD_DIM), lambda bi, gi: (s0_layer, bi, gi, 0, 0)),
                  per_head(c), per_head(c), per_head(c), per_head(1)],
        out_specs=(pl.BlockSpec((None, rows, w), lambda bi, gi: (bi, 0, gi)),
                   pl.BlockSpec((None, hps, HEAD_DIM, HEAD_DIM), lambda bi, gi: (bi, gi, 0, 0))),
        compiler_params=_cparams(("parallel", "parallel")),
        name="retention",
    )(z3, z3, z3, z3, cos, sin, s0, din, qd, kd, cd)


def _rope_tables(pos, rot_dim, theta):
    half = rot_dim // 2
    freqs = theta ** (-jnp.arange(half, dtype=F32) / half)
    ang = pos.astype(F32)[:, None] * freqs[None, :]
    cos, sin = jnp.cos(ang), jnp.sin(ang)
    n = pos.shape[0]
    rest = HEAD_DIM - rot_dim
    zh = jnp.zeros((n, half), F32)
    c = jnp.concatenate([cos, cos, jnp.ones((n, rest), F32)], axis=1)
    sa = jnp.concatenate([-sin, zh, jnp.zeros((n, rest), F32)], axis=1)
    sb = jnp.concatenate([zh, sin, jnp.zeros((n, rest), F32)], axis=1)
    return c, sa, sb


def _moba_rot(x, c, sa, sb):
    half = ROT_DIM // 2
    return x * c + pltpu.roll(x, HEAD_DIM - half, 1) * sa + pltpu.roll(x, half, 1) * sb


def _moba_prompt_body(mq_ref, mk_ref, mv_ref, mg_ref, c_ref, sa_ref, sb_ref, *refs, seq):
    o_ref, nk_ref, nv_ref = refs[-3:]
    blk = MOBA_BLOCK
    nb = seq // blk
    c, sa, sb = c_ref[...], sa_ref[...], sb_ref[...]
    q = _moba_rot(mq_ref[...].astype(F32), c, sa, sb)
    k = _moba_rot(mk_ref[...].astype(F32), c, sa, sb)
    v = mv_ref[...].astype(F32)
    if len(nk_ref.shape) == 4:
        for ref, val in ((nk_ref, k), (nv_ref, v)):
            ref[0] = val.reshape(seq // PAGE_SIZE, PAGE_SIZE, HEAD_DIM)
            ref[1:] = jnp.zeros((ref.shape[0] - 1,) + ref.shape[1:], F32)
    else:
        nk_ref[...] = k.reshape(seq // PAGE_SIZE, PAGE_SIZE, HEAD_DIM)
        nv_ref[...] = v.reshape(seq // PAGE_SIZE, PAGE_SIZE, HEAD_DIM)

    k_mean = jnp.mean(k.reshape(nb, blk, HEAD_DIM), axis=1)
    gate = lax.dot_general(k_mean, q, NT_DIMS, preferred_element_type=F32,
                           precision=lax.Precision.HIGHEST)
    qb_all = (q * (HEAD_DIM ** -0.5 * LOG2E)).astype(BF16)
    kb_all = k.astype(BF16)
    vt_all = jnp.concatenate([v.T, jnp.ones((2 * SUBLANES, seq), F32)], axis=0).astype(BF16)
    blk_id = lax.broadcasted_iota(jnp.int32, (nb, blk), 0)
    key_i = lax.broadcasted_iota(jnp.int32, (blk, blk), 0)
    qry_i = lax.broadcasted_iota(jnp.int32, (blk, blk), 1)
    causal = key_i <= qry_i

    for qi in range(nb):
        qs = slice(qi * blk, (qi + 1) * blk)
        qb = qb_all[qs]
        if qi > MOBA_TOP_K:
            g = gate[:, qs]
            rank = jnp.zeros((nb, blk), jnp.int32)
            for j in range(qi):
                gj = g[j:j + 1, :]
                ahead = (gj > g) | ((gj == g) & (j < blk_id))
                rank = rank + ahead.astype(jnp.int32)
            bias = jnp.where(rank < MOBA_TOP_K, 0.0, -jnp.inf)
        tiles = []
        for j in range(qi + 1):
            st = lax.dot_general(kb_all[j * blk:(j + 1) * blk], qb, NT_DIMS,
                                 preferred_element_type=F32)
            if j == qi:
                st = jnp.where(causal, st, -jnp.inf)
            elif qi > MOBA_TOP_K:
                st = st + bias[j:j + 1, :]
            tiles.append(st)
        st = tiles[0] if qi == 0 else jnp.concatenate(tiles, axis=0)
        p = jnp.exp2(st - jnp.max(st, axis=0, keepdims=True)).astype(BF16)
        acc = jnp.dot(vt_all[:, :(qi + 1) * blk], p, preferred_element_type=F32)
        o = (acc[:HEAD_DIM] / acc[HEAD_DIM:HEAD_DIM + 1]).T
        o_ref[qs, :] = (o * _silu(mg_ref[qs, :].astype(F32))).astype(o_ref.dtype)


N_RET_IN = 11
N_MOBA_IN = 7


def _mixer_body(*refs, seq, n_alias):
    n_in = N_RET_IN + N_MOBA_IN + n_alias
    ins, outs = refs[:n_in], refs[n_in:]
    _ret_body(*ins[:N_RET_IN], outs[0], outs[1], rows=seq, heads=1)
    _moba_prompt_body(*ins[N_RET_IN:N_RET_IN + N_MOBA_IN], outs[2], outs[3], outs[4], seq=seq)


def _mixer(layer, depth, z3, ret_cs, s0, ret_tables, moba_tabs, n_heads, kv_bufs):
    b, seq, _ = z3.shape
    din, qd, kd, cd = ret_tables
    c = RET_CHUNK
    col = lambda off: pl.BlockSpec((None, seq, HEAD_DIM), lambda bi, hi: (bi, 0, off * n_heads + hi))
    tab = pl.BlockSpec((seq, HEAD_DIM), lambda bi, hi: (0, 0))
    per_head = lambda r: pl.BlockSpec((1, r, LANES), lambda bi, hi: (hi, 0, 0))
    state = pl.BlockSpec((None, 1, HEAD_DIM, HEAD_DIM), lambda bi, hi: (bi, hi, 0, 0))
    n_pages = seq // PAGE_SIZE
    page_spec = pl.BlockSpec((None, None, n_pages, None, PAGE_SIZE, HEAD_DIM),
                             lambda bi, hi: (layer, bi, 0, hi, 0, 0))
    page_shape = jax.ShapeDtypeStruct((depth, b, n_pages, n_heads, PAGE_SIZE, HEAD_DIM), F32)
    in_specs = ([col(0), col(1), col(2), col(3), tab, tab,
                 pl.BlockSpec((None, None, 1, HEAD_DIM, HEAD_DIM), lambda bi, hi: (0, bi, hi, 0, 0)),
                 per_head(c), per_head(c), per_head(c), per_head(1)]
                + [col(4), col(5), col(6), col(7), tab, tab, tab])
    aliases = {}
    if kv_bufs:
        aliases = {len(in_specs): 3, len(in_specs) + 1: 4}
        in_specs = in_specs + [pl.BlockSpec(memory_space=pl.ANY)] * 2
    else:
        assert layer == 0
        page_spec = pl.BlockSpec((depth, None, n_pages, None, PAGE_SIZE, HEAD_DIM),
                                 lambda bi, hi: (0, bi, 0, hi, 0, 0))
    head_out = pl.BlockSpec((None, seq, HEAD_DIM), lambda bi, hi: (bi, 0, hi))
    mix_shape = jax.ShapeDtypeStruct((b, seq, n_heads * HEAD_DIM), MIX_DTYPE)
    return pl.pallas_call(
        functools.partial(_mixer_body, seq=seq, n_alias=len(kv_bufs)),
        out_shape=(mix_shape, jax.ShapeDtypeStruct((b, n_heads, HEAD_DIM, HEAD_DIM), F32),
                   mix_shape, page_shape, page_shape),
        grid=(b, n_heads),
        in_specs=in_specs,
        out_specs=(head_out, state, head_out, page_spec, page_spec),
        input_output_aliases=aliases,
        compiler_params=_cparams(("parallel", "parallel")),
        name="mixer",
    )(z3, z3, z3, z3, *ret_cs, s0, din, qd, kd, cd, z3, z3, z3, z3, *moba_tabs, *kv_bufs)


def _decode_gate_body(mq_ref, mk_ref, mv_ref, bm_ref, c_ref, sa_ref, sb_ref, q_ref, k_ref, v_ref, idx_ref,
                      *, n_heads, n_full):
    c, sa, sb = c_ref[...], sa_ref[...], sb_ref[...]
    rows = mq_ref.shape[0]
    blk_i = lax.broadcasted_iota(jnp.int32, (n_full, LANES), 0)
    pick_row = lax.broadcasted_iota(jnp.int32, (SUBLANES, LANES), 0)
    for h in range(n_heads):
        cs = slice(h * HEAD_DIM, (h + 1) * HEAD_DIM)
        q = _moba_rot(mq_ref[:, cs].astype(F32), c, sa, sb)
        k = _moba_rot(mk_ref[:, cs].astype(F32), c, sa, sb)
        q_ref[h] = q
        k_ref[h] = k
        v_ref[h] = mv_ref[:, cs].astype(F32)
        q_pad = jnp.concatenate([q, jnp.zeros((LANES - rows, HEAD_DIM), F32)], axis=0)
        bm_h = bm_ref[pl.ds(h, n_full, stride=n_heads), :]
        gate = lax.dot_general(bm_h, q_pad, NT_DIMS, preferred_element_type=F32,
                               precision=lax.Precision.HIGHEST)
        picks = jnp.zeros((SUBLANES, LANES), jnp.int32)
        for r in range(MOBA_TOP_K):
            m = jnp.max(gate, axis=0, keepdims=True)
            pick = jnp.min(jnp.where(gate == m, blk_i, n_full), axis=0, keepdims=True)
            picks = jnp.where(pick_row == r, pick, picks)
            gate = jnp.where(blk_i == pick, -jnp.inf, gate)
        idx_ref[h] = picks


def _decode_gate(z3, bm, tabs, n_ret, n_heads):
    b, rows, _ = z3.shape
    n_full = bm.shape[1]
    bm = bm.reshape(b, n_full * n_heads, HEAD_DIM)
    w = n_heads * HEAD_DIM
    base = 4 * n_ret * HEAD_DIM // w
    col = lambda off: pl.BlockSpec((None, rows, w), lambda bi: (bi, 0, base + off))
    tab = pl.BlockSpec((rows, HEAD_DIM), lambda bi: (0, 0))
    qkv_shape = jax.ShapeDtypeStruct((b, n_heads, rows, HEAD_DIM), F32)
    qkv_spec = pl.BlockSpec((None, n_heads, rows, HEAD_DIM), lambda bi: (bi, 0, 0, 0))
    return pl.pallas_call(
        functools.partial(_decode_gate_body, n_heads=n_heads, n_full=n_full),
        out_shape=(qkv_shape, qkv_shape, qkv_shape,
                   jax.ShapeDtypeStruct((b, n_heads, SUBLANES, LANES), jnp.int32)),
        grid=(b,),
        in_specs=[col(0), col(1), col(2),
                  pl.BlockSpec((None, n_full * n_heads, HEAD_DIM), lambda bi: (bi, 0, 0)),
                  tab, tab, tab],
        out_specs=(qkv_spec, qkv_spec, qkv_spec,
                   pl.BlockSpec((None, n_heads, SUBLANES, LANES), lambda bi: (bi, 0, 0, 0))),
        compiler_params=_cparams(("parallel",)),
        name="decode_gate",
    )(z3, z3, z3, bm, *tabs)


def _decode_attn_body(idx_ref, pt_ref, ck_ref, cv_ref, q_ref, k_ref, v_ref, mg_ref, o_ref,
                      kbuf_ref, vbuf_ref, ksem_ref, vsem_ref, *, layer, n_tok, heads, n_heads):
    ppb = MOBA_BLOCK // PAGE_SIZE
    per_tok = MOBA_TOP_K * ppb
    n_slab = n_tok * per_tok
    groups = n_heads // heads
    step = pl.program_id(0)
    n_steps = pl.num_programs(0)

    def start_gather(st, slot):
        bi = lax.div(st, groups)
        gi = lax.rem(st, groups)
        for hh in range(heads):
            hi = gi * heads + hh
            for t in range(n_tok):
                for r in range(MOBA_TOP_K):
                    blk = idx_ref[((bi * n_heads + hi) * MOBA_TOP_K + r) * n_tok + t]
                    for j in range(ppb):
                        pg = pt_ref[bi, blk * ppb + j]
                        dst = pl.ds((hh * n_slab + (t * MOBA_TOP_K + r) * ppb + j) * PAGE_SIZE, PAGE_SIZE)
                        pltpu.make_async_copy(ck_ref.at[layer, pg, hi], kbuf_ref.at[slot, dst],
                                              ksem_ref.at[slot]).start()
                        pltpu.make_async_copy(cv_ref.at[layer, pg, hi], vbuf_ref.at[slot, dst],
                                              vsem_ref.at[slot]).start()

    @pl.when(step == 0)
    def _():
        start_gather(step, 0)

    @pl.when(step + 1 < n_steps)
    def _():
        start_gather(step + 1, lax.rem(step + 1, 2))

    slot = lax.rem(step, 2)
    for i in range(heads * n_slab):
        dst = pl.ds(i * PAGE_SIZE, PAGE_SIZE)
        pltpu.make_async_copy(ck_ref.at[layer, 0, 0], kbuf_ref.at[slot, dst], ksem_ref.at[slot]).wait()
        pltpu.make_async_copy(cv_ref.at[layer, 0, 0], vbuf_ref.at[slot, dst], vsem_ref.at[slot]).wait()

    rows = q_ref.shape[1]
    tok_keys = per_tok * PAGE_SIZE
    pad = jnp.zeros((LANES - rows, HEAD_DIM), F32)
    tok_i = lax.broadcasted_iota(jnp.int32, (rows, LANES), 0)
    key_i = lax.broadcasted_iota(jnp.int32, (rows, LANES), 1)
    own_ok = (key_i <= tok_i) & (key_i < n_tok)
    row_lo = lax.broadcasted_iota(jnp.int32, (rows, n_tok * tok_keys), 0) * tok_keys
    col_i = lax.broadcasted_iota(jnp.int32, (rows, n_tok * tok_keys), 1)
    sel_ok = (col_i >= row_lo) & (col_i < row_lo + tok_keys)
    for hh in range(heads):
        span = pl.ds(hh * n_slab * PAGE_SIZE, n_slab * PAGE_SIZE)
        qb = (q_ref[hh] * (HEAD_DIM ** -0.5)).astype(BF16)
        ks = kbuf_ref[slot, span, :].astype(BF16)
        vs = vbuf_ref[slot, span, :].astype(BF16)
        s_sel = lax.dot_general(qb, ks, NT_DIMS, preferred_element_type=F32)
        s_sel = jnp.where(sel_ok, s_sel, -jnp.inf)
        k_own = jnp.concatenate([k_ref[hh], pad], axis=0).astype(BF16)
        v_own = jnp.concatenate([v_ref[hh], pad], axis=0).astype(BF16)
        s_own = lax.dot_general(qb, k_own, NT_DIMS, preferred_element_type=F32)
        s_own = jnp.where(own_ok, s_own, -jnp.inf)
        m = jnp.maximum(jnp.max(s_sel, axis=-1, keepdims=True), jnp.max(s_own, axis=-1, keepdims=True))
        p_sel = jnp.exp(s_sel - m)
        p_own = jnp.exp(s_own - m)
        l = jnp.sum(p_sel, axis=-1, keepdims=True) + jnp.sum(p_own, axis=-1, keepdims=True)
        o = (jnp.dot(p_sel.astype(BF16), vs, preferred_element_type=F32)
             + jnp.dot(p_own.astype(BF16), v_own, preferred_element_type=F32)) / l
        cs = slice(hh * HEAD_DIM, (hh + 1) * HEAD_DIM)
        o_ref[:, cs] = (o * _silu(mg_ref[:, cs].astype(F32))).astype(o_ref.dtype)


DECODE_HEADS_PER_STEP = 4


def _decode_attn(layer, cache_k, cache_v, idx_flat, page_table, q, k, v, z3, n_ret, n_tok):
    b, n_heads, rows, _ = q.shape
    ppb = MOBA_BLOCK // PAGE_SIZE
    hps = math.gcd(DECODE_HEADS_PER_STEP, n_heads)
    groups = n_heads // hps
    w = hps * HEAD_DIM
    buf_rows = hps * n_tok * MOBA_TOP_K * ppb * PAGE_SIZE
    tok_spec = pl.BlockSpec((None, hps, rows, HEAD_DIM), lambda s, idx, pt: (s // groups, s % groups, 0, 0))
    gate_col = (4 * n_ret + 3 * n_heads) // hps
    any_spec = pl.BlockSpec(memory_space=pl.ANY)
    return pl.pallas_call(
        functools.partial(_decode_attn_body, layer=layer, n_tok=n_tok, heads=hps, n_heads=n_heads),
        out_shape=jax.ShapeDtypeStruct((b, rows, n_heads * HEAD_DIM), MIX_DTYPE),
        grid_spec=pltpu.PrefetchScalarGridSpec(
            num_scalar_prefetch=2,
            grid=(b * groups,),
            in_specs=[any_spec, any_spec, tok_spec, tok_spec, tok_spec,
                      pl.BlockSpec((None, rows, w), lambda s, idx, pt: (s // groups, 0, gate_col + s % groups))],
            out_specs=pl.BlockSpec((None, rows, w), lambda s, idx, pt: (s // groups, 0, s % groups)),
            scratch_shapes=[pltpu.VMEM((2, buf_rows, HEAD_DIM), F32), pltpu.VMEM((2, buf_rows, HEAD_DIM), F32),
                            pltpu.SemaphoreType.DMA((2,)), pltpu.SemaphoreType.DMA((2,))],
        ),
        compiler_params=_cparams(("arbitrary",)),
        name="decode_attn",
    )(idx_flat, page_table, cache_k, cache_v, q, k, v, z3)


def _row_tile(m, cap):
    t = min(m, cap)
    while m % t:
        t //= 2
    return t


def kernel(x_prompt, x_sample, cache_k, cache_v, state_ret, page_table, p_prompt, p_sample,
           norm_mix, w_in, w_out, w_pe, w_pg, b_pg, norm_final):
    batch, seq, d_model = x_prompt.shape
    dec_batch, dec_seq, _ = x_sample.shape
    depth = w_in.shape[0]
    in_cols = w_in.shape[2]
    n_ret = in_cols // (8 * HEAD_DIM)
    n_moba = n_ret
    n_pages = page_table.shape[1]
    past = n_pages * PAGE_SIZE
    assert seq % MOBA_BLOCK == 0 and seq % RET_CHUNK == 0
    assert past % MOBA_BLOCK == 0 and past // MOBA_BLOCK >= MOBA_TOP_K
    assert dec_seq <= SUBLANES and RET_CHUNK % dec_seq == 0
    dec_rows = SUBLANES
    m_p = batch * seq
    m_s = dec_batch * dec_rows

    w_out_bf = w_out.astype(BF16)
    w_pg_bf = w_pg.astype(BF16)
    w_pe_bf = w_pe.astype(BF16)
    norms_in = norm_mix.reshape(depth, 1, d_model)
    norms_next = jnp.concatenate([norm_mix[1:], norm_final[None]], axis=0).reshape(depth, 1, d_model)
    bpg = b_pg.reshape(depth, 1, d_model)

    pos_p = jnp.arange(seq, dtype=jnp.int32)
    pos_s = past + jnp.arange(dec_rows, dtype=jnp.int32)
    ret_tab_p = _rope_tables(pos_p, HEAD_DIM, RET_THETA)
    ret_tab_s = _rope_tables(pos_s, HEAD_DIM, RET_THETA)
    ret_cs_p = (ret_tab_p[0], ret_tab_p[1] + ret_tab_p[2])
    ret_cs_s = (ret_tab_s[0], ret_tab_s[1] + ret_tab_s[2])
    moba_tab_p = _rope_tables(pos_p, ROT_DIM, ROPE_THETA)
    moba_tab_s = _rope_tables(pos_s, ROT_DIM, ROPE_THETA)
    dec_p = _ret_tables(n_ret, RET_CHUNK)
    dec_s = _ret_tables(n_ret, dec_seq)

    xp = x_prompt.reshape(m_p, d_model)
    xs = jnp.pad(x_sample, ((0, 0), (0, dec_rows - dec_seq), (0, 0))).reshape(m_s, d_model)
    pp = p_prompt.reshape(depth, m_p, -1)
    ps = jnp.pad(p_sample, ((0, 0), (0, 0), (0, dec_rows - dec_seq), (0, 0))).reshape(depth, m_s, -1)
    s0_prompt = jnp.zeros((1, batch, n_ret, HEAD_DIM, HEAD_DIM), F32)

    tm_in = _row_tile(m_p, 1024)
    tn = _row_tile(in_cols, 1024)
    tm_o = _row_tile(m_p, 256)
    pages_per_step = dec_batch * n_pages // ((in_cols // tn) * (m_p // tm_in))

    kv_bufs = ()
    hp = _prenorm(xp, norms_in, tm_o)
    hs = _prenorm(xs, norms_in, m_s)
    s_pr, k_sa, v_sa, s_sa = [], [], [], []
    for i in range(depth):
        last = i == depth - 1
        z, zs, block_means = _inproj(i, hp, hs, w_in, cache_k, page_table, tm_in, tn, pages_per_step)
        z = z.reshape(batch, seq, in_cols)
        zs = zs.reshape(dec_batch, dec_rows, in_cols)
        ret_o, s_new, att_o, *kv_bufs = _mixer(i, depth, z, ret_cs_p, s0_prompt, dec_p, moba_tab_p, n_moba, kv_bufs)
        s_pr.append(s_new)
        ret_s, ss_new = _retention(zs, *ret_cs_s, state_ret, i, dec_s, n_ret, n_ret)
        q, k, v, idx = _decode_gate(zs, block_means, moba_tab_s, n_ret, n_moba)
        idx_flat = idx[:, :, :MOBA_TOP_K, :dec_seq].reshape(-1)
        att_s = _decode_attn(i, cache_k, cache_v, idx_flat, page_table, q, k, v, zs, n_ret, dec_seq)
        outs = _outproj(i, xp, ret_o.reshape(m_p, -1), att_o.reshape(m_p, -1), pp,
                        xs, ret_s.reshape(m_s, -1), att_s.reshape(m_s, -1), ps,
                        w_out_bf, w_pg_bf, bpg, w_pe_bf, norms_next, tm_o, last)
        if last:
            xp, xs = outs
        else:
            xp, hp, xs, hs = outs
        k_sa.append(k[:, :, :dec_seq])
        v_sa.append(v[:, :, :dec_seq])
        s_sa.append(ss_new)

    y_prompt = xp.reshape(batch, seq, d_model)
    y_sample = xs.reshape(dec_batch, dec_rows, d_model)[:, :dec_seq]
    return (y_prompt, y_sample, kv_bufs[0], kv_bufs[1], jnp.stack(s_pr),
            jnp.stack(k_sa), jnp.stack(v_sa), jnp.stack(s_sa))
```

```python
import functools
import math

import jax
import jax.numpy as jnp
from jax import lax
from jax.experimental import pallas as pl
from jax.experimental.pallas import tpu as pltpu

HEAD_DIM = 128
PAGE_SIZE = 128
RET_CHUNK = 128
RET_THETA = 10000.0
MOBA_BLOCK = 256
MOBA_TOP_K = 3
ROPE_THETA = 500000.0
ROT_DIM = HEAD_DIM // 4
NORM_EPS = 1e-6
LANES = 128
SUBLANES = 8
VMEM_LIMIT = 56 * 1024 * 1024

F32 = jnp.float32
BF16 = jnp.bfloat16
NT_DIMS = (((1,), (1,)), ((), ()))
LOG2E = math.log2(math.e)

Z_DTYPE = jnp.bfloat16
MIX_DTYPE = jnp.bfloat16


def _cparams(sem):
    return pltpu.CompilerParams(dimension_semantics=sem, vmem_limit_bytes=VMEM_LIMIT)


def _rms(x, w):
    return x * lax.rsqrt(jnp.mean(x * x, axis=-1, keepdims=True) + NORM_EPS) * w


def _silu(g):
    return g * jax.nn.sigmoid(g)


def _prenorm_body(x_ref, nw_ref, h_ref):
    h_ref[...] = _rms(x_ref[...], nw_ref[...]).astype(h_ref.dtype)


def _prenorm(x, norms, tm):
    m, d = x.shape
    return pl.pallas_call(
        _prenorm_body,
        out_shape=jax.ShapeDtypeStruct((m, d), BF16),
        grid=(m // tm,),
        in_specs=[pl.BlockSpec((tm, d), lambda i: (i, 0)),
                  pl.BlockSpec((None, 1, d), lambda i: (0, 0, 0))],
        out_specs=pl.BlockSpec((tm, d), lambda i: (i, 0)),
        compiler_params=_cparams(("parallel",)),
        name="prenorm",
    )(x, norms)


def _sum_pages(page_refs, o_ref):
    ppb = MOBA_BLOCK // PAGE_SIZE
    for n in range(len(page_refs) // ppb):
        acc = jnp.sum(page_refs[ppb * n][...], axis=1)
        for r in range(1, ppb):
            acc = acc + jnp.sum(page_refs[ppb * n + r][...], axis=1)
        o_ref[n] = acc * (1.0 / MOBA_BLOCK)


def _inproj_body(pt_ref, h_ref, hs_ref, w_ref, ck_ref, z_ref, zs_ref, bm_ref, wbf_ref, pbuf_ref, sem_ref,
                 *, layer, g, n_i, steps_per_b):
    step = pl.program_id(0) * n_i + pl.program_id(1)
    n_steps = pl.num_programs(0) * n_i

    def page_copy(st, r, slot):
        pg = pt_ref[lax.div(st, steps_per_b), lax.rem(st, steps_per_b) * g + r]
        return pltpu.make_async_copy(ck_ref.at[layer, pg], pbuf_ref.at[slot, r], sem_ref.at[slot])

    @pl.when(step == 0)
    def _():
        for r in range(g):
            page_copy(step, r, 0).start()

    @pl.when(step + 1 < n_steps)
    def _():
        for r in range(g):
            page_copy(step + 1, r, lax.rem(step + 1, 2)).start()

    @pl.when(pl.program_id(1) == 0)
    def _():
        wbf_ref[...] = w_ref[...].astype(BF16)
        zs_ref[...] = jnp.dot(hs_ref[...], wbf_ref[...], preferred_element_type=F32).astype(zs_ref.dtype)

    z_ref[...] = jnp.dot(h_ref[...], wbf_ref[...], preferred_element_type=F32).astype(z_ref.dtype)

    slot = lax.rem(step, 2)
    for r in range(g):
        page_copy(step, r, slot).wait()
    _sum_pages([pbuf_ref.at[slot, r] for r in range(g)], bm_ref)


def _inproj(layer, h, hs, w_in, cache_k, page_table, tm, tn, g):
    m, d = h.shape
    ms = hs.shape[0]
    n = w_in.shape[2]
    _, _, heads, _, hd = cache_k.shape
    b, n_pages = page_table.shape
    ppb = MOBA_BLOCK // PAGE_SIZE
    n_i = m // tm
    steps_per_b = n_pages // g
    assert (n // tn) * n_i * g == b * n_pages and n_pages % g == 0 and g % ppb == 0

    def bm_map(j, i, pt):
        s = j * n_i + i
        return (s // steps_per_b, s % steps_per_b, 0, 0)

    return pl.pallas_call(
        functools.partial(_inproj_body, layer=layer, g=g, n_i=n_i, steps_per_b=steps_per_b),
        out_shape=(jax.ShapeDtypeStruct((m, n), Z_DTYPE), jax.ShapeDtypeStruct((ms, n), Z_DTYPE),
                   jax.ShapeDtypeStruct((b, n_pages // ppb, heads, hd), F32)),
        grid_spec=pltpu.PrefetchScalarGridSpec(
            num_scalar_prefetch=1,
            grid=(n // tn, n_i),
            in_specs=[
                pl.BlockSpec((tm, d), lambda j, i, pt: (i, 0)),
                pl.BlockSpec((ms, d), lambda j, i, pt: (0, 0)),
                pl.BlockSpec((None, d, tn), lambda j, i, pt: (layer, 0, j)),
                pl.BlockSpec(memory_space=pl.ANY),
            ],
            out_specs=(pl.BlockSpec((tm, tn), lambda j, i, pt: (i, j)),
                       pl.BlockSpec((ms, tn), lambda j, i, pt: (0, j)),
                       pl.BlockSpec((None, g // ppb, heads, hd), bm_map)),
            scratch_shapes=[pltpu.VMEM((d, tn), BF16),
                            pltpu.VMEM((2, g, heads, PAGE_SIZE, hd), F32),
                            pltpu.SemaphoreType.DMA((2,))],
        ),
        compiler_params=_cparams(("arbitrary", "arbitrary")),
        name="inproj",
    )(page_table, h, hs, w_in, cache_k)


def _outproj_body(x_ref, ret_ref, att_ref, p_ref, xs_ref, rets_ref, atts_ref, ps_ref,
                  wo_r_ref, wo_a_ref, wpg_ref, bpg_ref, wpe_ref, nw_ref, *out_refs, last):
    n_out = len(out_refs) // 2

    def mix(x_r, ret_r, att_r, p_r, outs):
        x1 = (x_r[...]
              + jnp.dot(ret_r[...], wo_r_ref[...], preferred_element_type=F32)
              + jnp.dot(att_r[...], wo_a_ref[...], preferred_element_type=F32))
        g = jax.nn.sigmoid(jnp.dot(x1.astype(BF16), wpg_ref[...], preferred_element_type=F32) + bpg_ref[...])
        pe = jnp.dot(p_r[...].astype(BF16), wpe_ref[...], preferred_element_type=F32)
        x2 = x1 + g * pe
        y = _rms(x2, nw_ref[...])
        if last:
            outs[0][...] = y
        else:
            outs[0][...] = x2
            outs[1][...] = y.astype(outs[1].dtype)

    @pl.when(pl.program_id(0) == 0)
    def _():
        mix(xs_ref, rets_ref, atts_ref, ps_ref, out_refs[n_out:])

    mix(x_ref, ret_ref, att_ref, p_ref, out_refs[:n_out])


def _outproj(layer, x, ret_m, att_m, p, xs, ret_s, att_s, ps, wo_bf, wpg_bf, bpg, wpe_bf, norms_next, tm, last):
    m, d = x.shape
    ms = xs.shape[0]
    rw = ret_m.shape[1]
    aw = att_m.shape[1]
    pd = p.shape[2]
    row = lambda w: pl.BlockSpec((tm, w), lambda i: (i, 0))
    whole = lambda w: pl.BlockSpec((ms, w), lambda i: (0, 0))
    lay = lambda r: pl.BlockSpec((None, r, d), lambda i: (layer, 0, 0))
    dtypes = [F32] if last else [F32, BF16]
    out_shape = [jax.ShapeDtypeStruct((m, d), t) for t in dtypes] + [jax.ShapeDtypeStruct((ms, d), t) for t in dtypes]
    out_specs = [row(d) for _ in dtypes] + [whole(d) for _ in dtypes]
    return pl.pallas_call(
        functools.partial(_outproj_body, last=last),
        out_shape=tuple(out_shape),
        grid=(m // tm,),
        in_specs=[
            row(d), row(rw), row(aw),
            pl.BlockSpec((None, tm, pd), lambda i: (layer, i, 0)),
            whole(d), whole(rw), whole(aw),
            pl.BlockSpec((None, ms, pd), lambda i: (layer, 0, 0)),
            lay(rw),
            pl.BlockSpec((None, aw, d), lambda i: (layer, rw // aw, 0)),
            lay(d), lay(1), lay(pd), lay(1),
        ],
        out_specs=tuple(out_specs),
        compiler_params=_cparams(("arbitrary",)),
        name="outproj",
    )(x, ret_m, att_m, p, xs, ret_s, att_s, ps, wo_bf, wo_bf, wpg_bf, bpg, wpe_bf, norms_next)


def _ret_tables(n_heads, c_eff):
    c = RET_CHUNK
    log_g = jnp.log(1.0 - 2.0 ** (-5.0 - jnp.arange(n_heads, dtype=F32)))
    i = jnp.arange(c, dtype=F32)
    rel = i[:, None] - i[None, :]
    din = jnp.where(rel >= 0, jnp.exp(jnp.maximum(rel, 0.0)[None] * log_g[:, None, None]), 0.0)
    qd = jnp.exp((i + 1.0)[None, :] * log_g[:, None])
    kd = jnp.where(i[None, :] < c_eff, jnp.exp((c_eff - 1.0 - i)[None, :] * log_g[:, None]), 0.0)
    cd = jnp.exp(c_eff * log_g)
    bc = lambda t: jnp.broadcast_to(t[..., None], t.shape + (LANES,))
    return din, bc(qd), bc(kd), jnp.broadcast_to(cd[:, None, None], (n_heads, 1, LANES))


def _ret_body(rq_ref, rk_ref, rv_ref, rg_ref, cos_ref, sin_ref, s0_ref, din_ref, qd_ref, kd_ref, cd_ref,
              o_ref, s_ref, *, rows, heads):
    c = RET_CHUNK
    n_chunks = max(rows // c, 1)
    padded = n_chunks * c

    def full(ref, cs):
        t = ref[:, cs].astype(F32)
        if rows < padded:
            t = jnp.concatenate([t, jnp.zeros((padded - rows, HEAD_DIM), F32)], axis=0)
        return t

    one = slice(0, HEAD_DIM)
    cos = full(cos_ref, one)
    sin = full(sin_ref, one)
    chunks = lambda t: t.reshape(n_chunks, c, HEAD_DIM)
    for hh in range(heads):
        cs = slice(hh * HEAD_DIM, (hh + 1) * HEAD_DIM)
        din, qd, kd, cd = din_ref[hh], qd_ref[hh], kd_ref[hh], cd_ref[hh]
        q = full(rq_ref, cs)
        k = full(rk_ref, cs)
        q = q * cos + pltpu.roll(q, HEAD_DIM // 2, 1) * sin
        k = (k * cos + pltpu.roll(k, HEAD_DIM // 2, 1) * sin) * (HEAD_DIM ** -0.5)
        qb = chunks(q).astype(BF16)
        kb = chunks(k).astype(BF16)
        qdb = (chunks(q) * qd[None]).astype(BF16)
        kdec = chunks(k) * kd[None]
        vb = chunks(full(rv_ref, cs)).astype(BF16)
        s = s0_ref[hh]
        outs = []
        for n in range(n_chunks):
            scores = lax.dot_general(qb[n], kb[n], NT_DIMS, preferred_element_type=F32) * din
            outs.append(jnp.dot(scores.astype(BF16), vb[n], preferred_element_type=F32)
                        + jnp.dot(qdb[n], s.astype(BF16), preferred_element_type=F32))
            s = s * cd + jnp.dot(kdec[n].T.astype(BF16), vb[n], preferred_element_type=F32)
        s_ref[hh] = s
        o = outs[0] if n_chunks == 1 else jnp.concatenate(outs, axis=0)
        mu = jnp.mean(o, axis=-1, keepdims=True)
        var = jnp.mean(jnp.square(o - mu), axis=-1, keepdims=True)
        o = (o - mu) * lax.rsqrt(var + NORM_EPS)
        o = o * _silu(full(rg_ref, cs))
        o_ref[:, cs] = o[:rows].astype(o_ref.dtype)


def _retention(z3, cos, sin, s0, s0_layer, tables, n_heads, heads_per_step):
    b, rows, _ = z3.shape
    din, qd, kd, cd = tables
    c = RET_CHUNK
    hps = heads_per_step
    w = hps * HEAD_DIM
    groups = n_heads // hps
    col = lambda off: pl.BlockSpec((None, rows, w), lambda bi, gi: (bi, 0, off * groups + gi))
    tab = pl.BlockSpec((rows, HEAD_DIM), lambda bi, gi: (0, 0))
    per_head = lambda r: pl.BlockSpec((hps, r, LANES), lambda bi, gi: (gi, 0, 0))
    return pl.pallas_call(
        functools.partial(_ret_body, rows=rows, heads=hps),
        out_shape=(jax.ShapeDtypeStruct((b, rows, n_heads * HEAD_DIM), MIX_DTYPE),
                   jax.ShapeDtypeStruct((b, n_heads, HEAD_DIM, HEAD_DIM), F32)),
        grid=(b, groups),
        in_specs=[col(0), col(1), col(2), col(3), tab, tab,
                  pl.BlockSpec((None, None, hps, HEAD_DIM, HEAD_DIM), lambda bi, gi: (s0_layer, bi, gi, 0, 0)),
                  per_head(c), per_head(c), per_head(c), per_head(1)],
        out_specs=(pl.BlockSpec((None, rows, w), lambda bi, gi: (bi, 0, gi)),
                   pl.BlockSpec((None, hps, HEAD_DIM, HEAD_DIM), lambda bi, gi: (bi, gi, 0, 0))),
        compiler_params=_cparams(("parallel", "parallel")),
        name="retention",
    )(z3, z3, z3, z3, cos, sin, s0, din, qd, kd, cd)


def _rope_tables(pos, rot_dim, theta):
    half = rot_dim // 2
    freqs = theta ** (-jnp.arange(half, dtype=F32) / half)
    ang = pos.astype(F32)[:, None] * freqs[None, :]
    cos, sin = jnp.cos(ang), jnp.sin(ang)
    n = pos.shape[0]
    rest = HEAD_DIM - rot_dim
    zh = jnp.zeros((n, half), F32)
    c = jnp.concatenate([cos, cos, jnp.ones((n, rest), F32)], axis=1)
    sa = jnp.concatenate([-sin, zh, jnp.zeros((n, rest), F32)], axis=1)
    sb = jnp.concatenate([zh, sin, jnp.zeros((n, rest), F32)], axis=1)
    return c, sa, sb


def _moba_rot(x, c, sa, sb):
    half = ROT_DIM // 2
    return x * c + pltpu.roll(x, HEAD_DIM - half, 1) * sa + pltpu.roll(x, half, 1) * sb


def _moba_prompt_body(mq_ref, mk_ref, mv_ref, mg_ref, c_ref, sa_ref, sb_ref, *refs, seq):
    o_ref, nk_ref, nv_ref = refs[-3:]
    blk = MOBA_BLOCK
    nb = seq // blk
    c, sa, sb = c_ref[...], sa_ref[...], sb_ref[...]
    q = _moba_rot(mq_ref[...].astype(F32), c, sa, sb)
    k = _moba_rot(mk_ref[...].astype(F32), c, sa, sb)
    v = mv_ref[...].astype(F32)
    if len(nk_ref.shape) == 4:
        for ref, val in ((nk_ref, k), (nv_ref, v)):
            ref[0] = val.reshape(seq // PAGE_SIZE, PAGE_SIZE, HEAD_DIM)
            ref[1:] = jnp.zeros((ref.shape[0] - 1,) + ref.shape[1:], F32)
    else:
        nk_ref[...] = k.reshape(seq // PAGE_SIZE, PAGE_SIZE, HEAD_DIM)
        nv_ref[...] = v.reshape(seq // PAGE_SIZE, PAGE_SIZE, HEAD_DIM)

    k_mean = jnp.mean(k.reshape(nb, blk, HEAD_DIM), axis=1)
    gate = lax.dot_general(k_mean, q, NT_DIMS, preferred_element_type=F32,
                           precision=lax.Precision.HIGHEST)
    qb_all = (q * (HEAD_DIM ** -0.5 * LOG2E)).astype(BF16)
    kb_all = k.astype(BF16)
    vt_all = jnp.concatenate([v.T, jnp.ones((2 * SUBLANES, seq), F32)], axis=0).astype(BF16)
    blk_id = lax.broadcasted_iota(jnp.int32, (nb, blk), 0)
    key_i = lax.broadcasted_iota(jnp.int32, (blk, blk), 0)
    qry_i = lax.broadcasted_iota(jnp.int32, (blk, blk), 1)
    causal = key_i <= qry_i

    for qi in range(nb):
        qs = slice(qi * blk, (qi + 1) * blk)
        qb = qb_all[qs]
        if qi > MOBA_TOP_K:
            g = gate[:, qs]
            rank = jnp.zeros((nb, blk), jnp.int32)
            for j in range(qi):
                gj = g[j:j + 1, :]
                ahead = (gj > g) | ((gj == g) & (j < blk_id))
                rank = rank + ahead.astype(jnp.int32)
            bias = jnp.where(rank < MOBA_TOP_K, 0.0, -jnp.inf)
        tiles = []
        for j in range(qi + 1):
            st = lax.dot_general(kb_all[j * blk:(j + 1) * blk], qb, NT_DIMS,
                                 preferred_element_type=F32)
            if j == qi:
                st = jnp.where(causal, st, -jnp.inf)
            elif qi > MOBA_TOP_K:
                st = st + bias[j:j + 1, :]
            tiles.append(st)
        st = tiles[0] if qi == 0 else jnp.concatenate(tiles, axis=0)
        p = jnp.exp2((st - jnp.max(st, axis=0, keepdims=True)).astype(BF16))
        acc = jnp.dot(vt_all[:, :(qi + 1) * blk], p, preferred_element_type=F32)
        o = (acc[:HEAD_DIM] / acc[HEAD_DIM:HEAD_DIM + 1]).T
        o_ref[qs, :] = (o * _silu(mg_ref[qs, :].astype(F32))).astype(o_ref.dtype)


N_RET_IN = 11
N_MOBA_IN = 7


def _mixer_body(*refs, seq, n_alias):
    n_in = N_RET_IN + N_MOBA_IN + n_alias
    ins, outs = refs[:n_in], refs[n_in:]
    _ret_body(*ins[:N_RET_IN], outs[0], outs[1], rows=seq, heads=1)
    _moba_prompt_body(*ins[N_RET_IN:N_RET_IN + N_MOBA_IN], outs[2], outs[3], outs[4], seq=seq)


def _mixer(layer, depth, z3, ret_cs, s0, ret_tables, moba_tabs, n_heads, kv_bufs):
    b, seq, _ = z3.shape
    din, qd, kd, cd = ret_tables
    c = RET_CHUNK
    col = lambda off: pl.BlockSpec((None, seq, HEAD_DIM), lambda bi, hi: (bi, 0, off * n_heads + hi))
    tab = pl.BlockSpec((seq, HEAD_DIM), lambda bi, hi: (0, 0))
    per_head = lambda r: pl.BlockSpec((1, r, LANES), lambda bi, hi: (hi, 0, 0))
    state = pl.BlockSpec((None, 1, HEAD_DIM, HEAD_DIM), lambda bi, hi: (bi, hi, 0, 0))
    n_pages = seq // PAGE_SIZE
    page_spec = pl.BlockSpec((None, None, n_pages, None, PAGE_SIZE, HEAD_DIM),
                             lambda bi, hi: (layer, bi, 0, hi, 0, 0))
    page_shape = jax.ShapeDtypeStruct((depth, b, n_pages, n_heads, PAGE_SIZE, HEAD_DIM), F32)
    in_specs = ([col(0), col(1), col(2), col(3), tab, tab,
                 pl.BlockSpec((None, None, 1, HEAD_DIM, HEAD_DIM), lambda bi, hi: (0, bi, hi, 0, 0)),
                 per_head(c), per_head(c), per_head(c), per_head(1)]
                + [col(4), col(5), col(6), col(7), tab, tab, tab])
    aliases = {}
    if kv_bufs:
        aliases = {len(in_specs): 3, len(in_specs) + 1: 4}
        in_specs = in_specs + [pl.BlockSpec(memory_space=pl.ANY)] * 2
    else:
        assert layer == 0
        page_spec = pl.BlockSpec((depth, None, n_pages, None, PAGE_SIZE, HEAD_DIM),
                                 lambda bi, hi: (0, bi, 0, hi, 0, 0))
    head_out = pl.BlockSpec((None, seq, HEAD_DIM), lambda bi, hi: (bi, 0, hi))
    mix_shape = jax.ShapeDtypeStruct((b, seq, n_heads * HEAD_DIM), MIX_DTYPE)
    return pl.pallas_call(
        functools.partial(_mixer_body, seq=seq, n_alias=len(kv_bufs)),
        out_shape=(mix_shape, jax.ShapeDtypeStruct((b, n_heads, HEAD_DIM, HEAD_DIM), F32),
                   mix_shape, page_shape, page_shape),
        grid=(b, n_heads),
        in_specs=in_specs,
        out_specs=(head_out, state, head_out, page_spec, page_spec),
        input_output_aliases=aliases,
        compiler_params=_cparams(("parallel", "parallel")),
        name="mixer",
    )(z3, z3, z3, z3, *ret_cs, s0, din, qd, kd, cd, z3, z3, z3, z3, *moba_tabs, *kv_bufs)


def _decode_gate_body(mq_ref, mk_ref, mv_ref, bm_ref, c_ref, sa_ref, sb_ref, q_ref, k_ref, v_ref, idx_ref,
                      *, n_heads, n_full):
    c, sa, sb = c_ref[...], sa_ref[...], sb_ref[...]
    rows = mq_ref.shape[0]
    blk_i = lax.broadcasted_iota(jnp.int32, (n_full, LANES), 0)
    pick_row = lax.broadcasted_iota(jnp.int32, (SUBLANES, LANES), 0)
    for h in range(n_heads):
        cs = slice(h * HEAD_DIM, (h + 1) * HEAD_DIM)
        q = _moba_rot(mq_ref[:, cs].astype(F32), c, sa, sb)
        k = _moba_rot(mk_ref[:, cs].astype(F32), c, sa, sb)
        q_ref[h] = q
        k_ref[h] = k
        v_ref[h] = mv_ref[:, cs].astype(F32)
        q_pad = jnp.concatenate([q, jnp.zeros((LANES - rows, HEAD_DIM), F32)], axis=0)
        bm_h = bm_ref[pl.ds(h, n_full, stride=n_heads), :]
        gate = lax.dot_general(bm_h, q_pad, NT_DIMS, preferred_element_type=F32,
                               precision=lax.Precision.HIGHEST)
        picks = jnp.zeros((SUBLANES, LANES), jnp.int32)
        for r in range(MOBA_TOP_K):
            m = jnp.max(gate, axis=0, keepdims=True)
            pick = jnp.min(jnp.where(gate == m, blk_i, n_full), axis=0, keepdims=True)
            picks = jnp.where(pick_row == r, pick, picks)
            gate = jnp.where(blk_i == pick, -jnp.inf, gate)
        idx_ref[h] = picks


def _decode_gate(z3, bm, tabs, n_ret, n_heads):
    b, rows, _ = z3.shape
    n_full = bm.shape[1]
    bm = bm.reshape(b, n_full * n_heads, HEAD_DIM)
    w = n_heads * HEAD_DIM
    base = 4 * n_ret * HEAD_DIM // w
    col = lambda off: pl.BlockSpec((None, rows, w), lambda bi: (bi, 0, base + off))
    tab = pl.BlockSpec((rows, HEAD_DIM), lambda bi: (0, 0))
    qkv_shape = jax.ShapeDtypeStruct((b, n_heads, rows, HEAD_DIM), F32)
    qkv_spec = pl.BlockSpec((None, n_heads, rows, HEAD_DIM), lambda bi: (bi, 0, 0, 0))
    return pl.pallas_call(
        functools.partial(_decode_gate_body, n_heads=n_heads, n_full=n_full),
        out_shape=(qkv_shape, qkv_shape, qkv_shape,
                   jax.ShapeDtypeStruct((b, n_heads, SUBLANES, LANES), jnp.int32)),
        grid=(b,),
        in_specs=[col(0), col(1), col(2),
                  pl.BlockSpec((None, n_full * n_heads, HEAD_DIM), lambda bi: (bi, 0, 0)),
                  tab, tab, tab],
        out_specs=(qkv_spec, qkv_spec, qkv_spec,
                   pl.BlockSpec((None, n_heads, SUBLANES, LANES), lambda bi: (bi, 0, 0, 0))),
        compiler_params=_cparams(("parallel",)),
        name="decode_gate",
    )(z3, z3, z3, bm, *tabs)


def _decode_attn_body(idx_ref, pt_ref, ck_ref, cv_ref, q_ref, k_ref, v_ref, mg_ref, o_ref,
                      kbuf_ref, vbuf_ref, ksem_ref, vsem_ref, *, layer, n_tok, heads, n_heads):
    ppb = MOBA_BLOCK // PAGE_SIZE
    per_tok = MOBA_TOP_K * ppb
    n_slab = n_tok * per_tok
    groups = n_heads // heads
    step = pl.program_id(0)
    n_steps = pl.num_programs(0)

    def start_gather(st, slot):
        bi = lax.div(st, groups)
        gi = lax.rem(st, groups)
        for hh in range(heads):
            hi = gi * heads + hh
            for t in range(n_tok):
                for r in range(MOBA_TOP_K):
                    blk = idx_ref[((bi * n_heads + hi) * MOBA_TOP_K + r) * n_tok + t]
                    for j in range(ppb):
                        pg = pt_ref[bi, blk * ppb + j]
                        dst = pl.ds((hh * n_slab + (t * MOBA_TOP_K + r) * ppb + j) * PAGE_SIZE, PAGE_SIZE)
                        pltpu.make_async_copy(ck_ref.at[layer, pg, hi], kbuf_ref.at[slot, dst],
                                              ksem_ref.at[slot]).start()
                        pltpu.make_async_copy(cv_ref.at[layer, pg, hi], vbuf_ref.at[slot, dst],
                                              vsem_ref.at[slot]).start()

    @pl.when(step == 0)
    def _():
        start_gather(step, 0)

    @pl.when(step + 1 < n_steps)
    def _():
        start_gather(step + 1, lax.rem(step + 1, 2))

    slot = lax.rem(step, 2)
    for i in range(heads * n_slab):
        dst = pl.ds(i * PAGE_SIZE, PAGE_SIZE)
        pltpu.make_async_copy(ck_ref.at[layer, 0, 0], kbuf_ref.at[slot, dst], ksem_ref.at[slot]).wait()
        pltpu.make_async_copy(cv_ref.at[layer, 0, 0], vbuf_ref.at[slot, dst], vsem_ref.at[slot]).wait()

    rows = q_ref.shape[1]
    tok_keys = per_tok * PAGE_SIZE
    pad = jnp.zeros((LANES - rows, HEAD_DIM), F32)
    tok_i = lax.broadcasted_iota(jnp.int32, (rows, LANES), 0)
    key_i = lax.broadcasted_iota(jnp.int32, (rows, LANES), 1)
    own_ok = (key_i <= tok_i) & (key_i < n_tok)
    row_lo = lax.broadcasted_iota(jnp.int32, (rows, n_tok * tok_keys), 0) * tok_keys
    col_i = lax.broadcasted_iota(jnp.int32, (rows, n_tok * tok_keys), 1)
    sel_ok = (col_i >= row_lo) & (col_i < row_lo + tok_keys)
    for hh in range(heads):
        span = pl.ds(hh * n_slab * PAGE_SIZE, n_slab * PAGE_SIZE)
        qb = (q_ref[hh] * (HEAD_DIM ** -0.5)).astype(BF16)
        ks = kbuf_ref[slot, span, :].astype(BF16)
        vs = vbuf_ref[slot, span, :].astype(BF16)
        s_sel = lax.dot_general(qb, ks, NT_DIMS, preferred_element_type=F32)
        s_sel = jnp.where(sel_ok, s_sel, -jnp.inf)
        k_own = jnp.concatenate([k_ref[hh], pad], axis=0).astype(BF16)
        v_own = jnp.concatenate([v_ref[hh], pad], axis=0).astype(BF16)
        s_own = lax.dot_general(qb, k_own, NT_DIMS, preferred_element_type=F32)
        s_own = jnp.where(own_ok, s_own, -jnp.inf)
        m = jnp.maximum(jnp.max(s_sel, axis=-1, keepdims=True), jnp.max(s_own, axis=-1, keepdims=True))
        p_sel = jnp.exp(s_sel - m)
        p_own = jnp.exp(s_own - m)
        l = jnp.sum(p_sel, axis=-1, keepdims=True) + jnp.sum(p_own, axis=-1, keepdims=True)
        o = (jnp.dot(p_sel.astype(BF16), vs, preferred_element_type=F32)
             + jnp.dot(p_own.astype(BF16), v_own, preferred_element_type=F32)) / l
        cs = slice(hh * HEAD_DIM, (hh + 1) * HEAD_DIM)
        o_ref[:, cs] = (o * _silu(mg_ref[:, cs].astype(F32))).astype(o_ref.dtype)


DECODE_HEADS_PER_STEP = 4


def _decode_attn(layer, cache_k, cache_v, idx_flat, page_table, q, k, v, z3, n_ret, n_tok):
    b, n_heads, rows, _ = q.shape
    ppb = MOBA_BLOCK // PAGE_SIZE
    hps = math.gcd(DECODE_HEADS_PER_STEP, n_heads)
    groups = n_heads // hps
    w = hps * HEAD_DIM
    buf_rows = hps * n_tok * MOBA_TOP_K * ppb * PAGE_SIZE
    tok_spec = pl.BlockSpec((None, hps, rows, HEAD_DIM), lambda s, idx, pt: (s // groups, s % groups, 0, 0))
    gate_col = (4 * n_ret + 3 * n_heads) // hps
    any_spec = pl.BlockSpec(memory_space=pl.ANY)
    return pl.pallas_call(
        functools.partial(_decode_attn_body, layer=layer, n_tok=n_tok, heads=hps, n_heads=n_heads),
        out_shape=jax.ShapeDtypeStruct((b, rows, n_heads * HEAD_DIM), MIX_DTYPE),
        grid_spec=pltpu.PrefetchScalarGridSpec(
            num_scalar_prefetch=2,
            grid=(b * groups,),
            in_specs=[any_spec, any_spec, tok_spec, tok_spec, tok_spec,
                      pl.BlockSpec((None, rows, w), lambda s, idx, pt: (s // groups, 0, gate_col + s % groups))],
            out_specs=pl.BlockSpec((None, rows, w), lambda s, idx, pt: (s // groups, 0, s % groups)),
            scratch_shapes=[pltpu.VMEM((2, buf_rows, HEAD_DIM), F32), pltpu.VMEM((2, buf_rows, HEAD_DIM), F32),
                            pltpu.SemaphoreType.DMA((2,)), pltpu.SemaphoreType.DMA((2,))],
        ),
        compiler_params=_cparams(("arbitrary",)),
        name="decode_attn",
    )(idx_flat, page_table, cache_k, cache_v, q, k, v, z3)


def _row_tile(m, cap):
    t = min(m, cap)
    while m % t:
        t //= 2
    return t


def kernel(x_prompt, x_sample, cache_k, cache_v, state_ret, page_table, p_prompt, p_sample,
           norm_mix, w_in, w_out, w_pe, w_pg, b_pg, norm_final):
    batch, seq, d_model = x_prompt.shape
    dec_batch, dec_seq, _ = x_sample.shape
    depth = w_in.shape[0]
    in_cols = w_in.shape[2]
    n_ret = in_cols // (8 * HEAD_DIM)
    n_moba = n_ret
    n_pages = page_table.shape[1]
    past = n_pages * PAGE_SIZE
    assert seq % MOBA_BLOCK == 0 and seq % RET_CHUNK == 0
    assert past % MOBA_BLOCK == 0 and past // MOBA_BLOCK >= MOBA_TOP_K
    assert dec_seq <= SUBLANES and RET_CHUNK % dec_seq == 0
    dec_rows = SUBLANES
    m_p = batch * seq
    m_s = dec_batch * dec_rows

    w_out_bf = w_out.astype(BF16)
    w_pg_bf = w_pg.astype(BF16)
    w_pe_bf = w_pe.astype(BF16)
    norms_in = norm_mix.reshape(depth, 1, d_model)
    norms_next = jnp.concatenate([norm_mix[1:], norm_final[None]], axis=0).reshape(depth, 1, d_model)
    bpg = b_pg.reshape(depth, 1, d_model)

    pos_p = jnp.arange(seq, dtype=jnp.int32)
    pos_s = past + jnp.arange(dec_rows, dtype=jnp.int32)
    ret_tab_p = _rope_tables(pos_p, HEAD_DIM, RET_THETA)
    ret_tab_s = _rope_tables(pos_s, HEAD_DIM, RET_THETA)
    ret_cs_p = (ret_tab_p[0], ret_tab_p[1] + ret_tab_p[2])
    ret_cs_s = (ret_tab_s[0], ret_tab_s[1] + ret_tab_s[2])
    moba_tab_p = _rope_tables(pos_p, ROT_DIM, ROPE_THETA)
    moba_tab_s = _rope_tables(pos_s, ROT_DIM, ROPE_THETA)
    dec_p = _ret_tables(n_ret, RET_CHUNK)
    dec_s = _ret_tables(n_ret, dec_seq)

    xp = x_prompt.reshape(m_p, d_model)
    xs = jnp.pad(x_sample, ((0, 0), (0, dec_rows - dec_seq), (0, 0))).reshape(m_s, d_model)
    pp = p_prompt.reshape(depth, m_p, -1)
    ps = jnp.pad(p_sample, ((0, 0), (0, 0), (0, dec_rows - dec_seq), (0, 0))).reshape(depth, m_s, -1)
    s0_prompt = jnp.zeros((1, batch, n_ret, HEAD_DIM, HEAD_DIM), F32)

    tm_in = _row_tile(m_p, 1024)
    tn = _row_tile(in_cols, 1024)
    tm_o = _row_tile(m_p, 256)
    pages_per_step = dec_batch * n_pages // ((in_cols // tn) * (m_p // tm_in))

    kv_bufs = ()
    hp = _prenorm(xp, norms_in, tm_o)
    hs = _prenorm(xs, norms_in, m_s)
    s_pr, k_sa, v_sa, s_sa = [], [], [], []
    for i in range(depth):
        last = i == depth - 1
        z, zs, block_means = _inproj(i, hp, hs, w_in, cache_k, page_table, tm_in, tn, pages_per_step)
        z = z.reshape(batch, seq, in_cols)
        zs = zs.reshape(dec_batch, dec_rows, in_cols)
        ret_o, s_new, att_o, *kv_bufs = _mixer(i, depth, z, ret_cs_p, s0_prompt, dec_p, moba_tab_p, n_moba, kv_bufs)
        s_pr.append(s_new)
        ret_s, ss_new = _retention(zs, *ret_cs_s, state_ret, i, dec_s, n_ret, n_ret)
        q, k, v, idx = _decode_gate(zs, block_means, moba_tab_s, n_ret, n_moba)
        idx_flat = idx[:, :, :MOBA_TOP_K, :dec_seq].reshape(-1)
        att_s = _decode_attn(i, cache_k, cache_v, idx_flat, page_table, q, k, v, zs, n_ret, dec_seq)
        outs = _outproj(i, xp, ret_o.reshape(m_p, -1), att_o.reshape(m_p, -1), pp,
                        xs, ret_s.reshape(m_s, -1), att_s.reshape(m_s, -1), ps,
                        w_out_bf, w_pg_bf, bpg, w_pe_bf, norms_next, tm_o, last)
        if last:
            xp, xs = outs
        else:
            xp, hp, xs, hs = outs
        k_sa.append(k[:, :, :dec_seq])
        v_sa.append(v[:, :, :dec_seq])
        s_sa.append(ss_new)

    y_prompt = xp.reshape(batch, seq, d_model)
    y_sample = xs.reshape(dec_batch, dec_rows, d_model)[:, :dec_seq]
    return (y_prompt, y_sample, kv_bufs[0], kv_bufs[1], jnp.stack(s_pr),
            jnp.stack(k_sa), jnp.stack(v_sa), jnp.stack(s_sa))
```
